```python
import jax, jax.numpy as jnp
from jax import lax
import numpy as np

D_MODEL = 1024
BATCH = 16
SEQ = 256
DEPTH = 4
DEC_BATCH = 8
DEC_SEQ = 2048
PAST_LEN = 512

GRID_W = 64
GLA_H = 4
GLA_DK = 64
GLA_DV = 128
GLA_RANK = 16
GLA_GATE_NORM = 16.0
HG_H = 4
HG_DK = 128
HG_DV = 128
GM_G = 4
GM_CH = 128
GM_CHUNK = 128
SCAN_CHUNK = 64
D_FF = 4 * D_MODEL
EPS = 1e-6
GLA_QK = GLA_H * GLA_DK
GLA_V = GLA_H * GLA_DV
HG_K = HG_H * HG_DK
HG_V = HG_H * HG_DV
GM_W = GM_G * GM_CH
SPLIT_SIZES = (GLA_QK, GLA_QK, GLA_V, GLA_V, GLA_RANK, GLA_RANK,
               HG_K, HG_K, HG_K, HG_V, HG_V,
               GM_W, GM_W,
               D_MODEL, D_MODEL, D_MODEL)
SPLIT_POINTS = tuple(int(s) for s in np.cumsum(SPLIT_SIZES)[:-1])
D_IN = int(sum(SPLIT_SIZES))

kernel_name = "hybrid_gla_hgrn2_gmlp_diffusion_step"

F32 = jnp.float32


def rms_norm(x, g):
    xf = x.astype(F32)
    xf = xf * lax.rsqrt(jnp.mean(xf * xf, axis=-1, keepdims=True) + EPS)
    return xf.astype(x.dtype) * g


def split_heads(t, n):
    b, s, _ = t.shape
    return t.reshape(b, s, n, -1).transpose(0, 2, 1, 3)


def head_norm_gate(o, g, gate):
    b, h, s, dv = o.shape
    of = o.astype(F32)
    of = of * lax.rsqrt(jnp.mean(of * of, axis=-1, keepdims=True) + EPS)
    of = of.transpose(0, 2, 1, 3).reshape(b, s, h * dv).astype(gate.dtype)
    return of * g * jax.nn.silu(gate)


def chunked_gla(q, k, v, log_a, s0):
    b, h, s, _ = q.shape
    n = s // SCAN_CHUNK

    def to_chunks(t):
        return jnp.moveaxis(t.reshape(b, h, n, SCAN_CHUNK, t.shape[-1]), 2, 0)

    causal = jnp.tril(jnp.ones((SCAN_CHUNK, SCAN_CHUNK), dtype=bool))

    def step(st, inp):
        qc, kc, vc, ac = inp
        cum = jnp.cumsum(ac.astype(F32), axis=2)
        rel = cum[:, :, :, None, :] - cum[:, :, None, :, :]
        decay = jnp.exp(jnp.where(causal[:, :, None], rel, -jnp.inf))
        scores = jnp.einsum('bhid,bhjd,bhijd->bhij', qc, kc, decay)
        o = (jnp.einsum('bhij,bhjv->bhiv', scores, vc)
             + jnp.einsum('bhid,bhdv->bhiv', qc * jnp.exp(cum), st))
        last = cum[:, :, -1:, :]
        st_new = (jnp.exp(last[:, :, 0, :])[..., None] * st
                  + jnp.einsum('bhjd,bhjv->bhdv', kc * jnp.exp(last - cum), vc))
        return st_new.astype(st.dtype), o.astype(v.dtype)

    s_fin, o = lax.scan(step, s0, (to_chunks(q), to_chunks(k), to_chunks(v), to_chunks(log_a)))
    o = jnp.moveaxis(o, 0, 2).reshape(b, h, s, v.shape[-1])
    return o, s_fin


def bidir_scan(q, k_f, k_b, v, la_f, la_b, s0_f, s0_b):
    o_f, s_f = chunked_gla(q, k_f, v, la_f, s0_f)
    flip = lambda t: jnp.flip(t, axis=2)
    o_b, s_b = chunked_gla(flip(q), flip(k_b), flip(v), flip(la_b), s0_b)
    return o_f + flip(o_b), jnp.stack([s_f, s_b], axis=1)


def hgrn_lower_bounds(logits):
    p = jax.nn.softmax(logits.astype(F32), axis=1)
    cum = jnp.cumsum(p, axis=1)
    return cum - cum[:, :1]


def chunk_spatial_gate(u, v, g, ws, bs):
    b, s, _ = u.shape
    n = s // GM_CHUNK
    vf = v.astype(F32)
    mu = jnp.mean(vf, axis=-1, keepdims=True)
    var = jnp.mean(jnp.square(vf - mu), axis=-1, keepdims=True)
    vn = ((vf - mu) * lax.rsqrt(var + EPS)).astype(v.dtype) * g
    vc = vn.reshape(b, n, GM_CHUNK, GM_G, GM_CH)
    mixed = jnp.einsum('gpq,bnqgc->bnpgc', ws, vc) + bs.T[:, :, None]
    return u * mixed.reshape(b, s, GM_W)


def grid_posemb(n_tok, dtype):
    rows = n_tok // GRID_W
    r, col = jnp.meshgrid(jnp.arange(rows, dtype=F32), jnp.arange(GRID_W, dtype=F32), indexing='ij')
    r = r.reshape(-1)
    col = col.reshape(-1)
    nf = D_MODEL // 4
    omega = 1.0 / (10000.0 ** (jnp.arange(nf, dtype=F32) / nf))
    er = r[:, None] * omega
    ec = col[:, None] * omega
    return jnp.concatenate([jnp.sin(er), jnp.cos(er), jnp.sin(ec), jnp.cos(ec)], axis=-1).astype(dtype)


def token_mixers(h, s_gla, s_hg, p, lb_f, lb_b):
    proj = h @ p['w_in']
    (gq, gk, gv, go, gzf, gzb, hq, hff, hfb, hi, ho, mu, mv, a_gla, a_hg, a_gm) = jnp.split(
        proj, SPLIT_POINTS, axis=-1)
    q = split_heads(gq, GLA_H) * (GLA_DK ** -0.5)
    k = split_heads(gk, GLA_H)
    v = split_heads(gv, GLA_H)
    la_f = split_heads(jax.nn.log_sigmoid((gzf @ p['gla_lr_w'][0] + p['gla_lr_b'][0]).astype(F32)) / GLA_GATE_NORM, GLA_H)
    la_b = split_heads(jax.nn.log_sigmoid((gzb @ p['gla_lr_w'][1] + p['gla_lr_b'][1]).astype(F32)) / GLA_GATE_NORM, GLA_H)
    o, s_gla_new = bidir_scan(q, k, k, v, la_f, la_b, s_gla[:, 0], s_gla[:, 1])
    o_gla = head_norm_gate(o, p['gla_norm_g'], go)
    q = split_heads(hq, HG_H) * (HG_DK ** -0.5)
    hff32 = hff.astype(F32)
    hfb32 = hfb.astype(F32)
    lf_f = jnp.log(lb_f + (1.0 - lb_f) * jax.nn.sigmoid(hff32))
    lf_b = jnp.log(lb_b + (1.0 - lb_b) * jax.nn.sigmoid(hfb32))
    k_f = (1.0 - lb_f) * jax.nn.sigmoid(-hff32)
    k_b = (1.0 - lb_b) * jax.nn.sigmoid(-hfb32)
    v = split_heads(hi, HG_H)
    o, s_hg_new = bidir_scan(q, split_heads(k_f, HG_H), split_heads(k_b, HG_H), v,
                             split_heads(lf_f, HG_H), split_heads(lf_b, HG_H), s_hg[:, 0], s_hg[:, 1])
    o_hg = head_norm_gate(o, p['hg_norm_g'], ho)
    o_gm = chunk_spatial_gate(jax.nn.gelu(mu), jax.nn.gelu(mv), p['gm_norm_g'], p['gm_ws'], p['gm_bs'])
    merged = (jax.nn.sigmoid(a_gla) * (o_gla @ p['w_br_gla'])
              + jax.nn.sigmoid(a_hg) * (o_hg @ p['w_br_hg'])
              + jax.nn.sigmoid(a_gm) * (o_gm @ p['w_br_gm']))
    return merged @ p['w_out'], s_gla_new, s_hg_new


def trunk_layer(x, mod, s_gla, s_hg, lb_f, lb_b, p):
    sh1, sc1, gt1, sh2, sc2, gt2 = jnp.split(mod, 6, axis=-1)
    h = rms_norm(x, p['norm_mix_g']) * (1.0 + sc1) + sh1
    mix, s_gla_new, s_hg_new = token_mixers(h, s_gla, s_hg, p, lb_f, lb_b)
    x = x + gt1 * mix
    h = rms_norm(x, p['norm_ffn_g']) * (1.0 + sc2) + sh2
    x = x + gt2 * (jnp.square(jax.nn.relu(h @ p['w_ff1'])) @ p['w_ff2'])
    return x, s_gla_new, s_hg_new


def setup_inputs(seed: int = 0) -> dict:
    key = jax.random.key(seed)
    ks = jax.random.split(key, 26)
    nrm = lambda k, shape, scale: jax.random.normal(k, shape, F32) * scale
    gain = lambda k, shape: 1.0 + 0.02 * jax.random.normal(k, shape, F32)
    D = D_MODEL
    return {
        'x_prompt': nrm(ks[0], (BATCH, SEQ, D), 1.0),
        'x_sample': nrm(ks[1], (DEC_BATCH, DEC_SEQ, D), 1.0),
        'c': nrm(ks[2], (DEC_BATCH, D), 1.0),
        'state_gla': nrm(ks[3], (DEC_BATCH, DEPTH, 2, GLA_H, GLA_DK, GLA_DV), 0.3),
        'state_hgrn': nrm(ks[4], (DEC_BATCH, DEPTH, 2, HG_H, HG_DK, HG_DV), 0.3),
        'c_ctx': nrm(ks[5], (D,), 1.0),
        'ada_w': nrm(ks[6], (DEPTH, D, 6 * D), 0.5 * D ** -0.5),
        'ada_b': nrm(ks[7], (DEPTH, 6 * D), 0.02),
        'norm_mix_g': gain(ks[8], (DEPTH, D)),
        'norm_ffn_g': gain(ks[9], (DEPTH, D)),
        'w_in': nrm(ks[10], (DEPTH, D, D_IN), D ** -0.5),
        'gla_lr_w': nrm(ks[11], (DEPTH, 2, GLA_RANK, GLA_QK), GLA_RANK ** -0.5),
        'gla_lr_b': nrm(ks[12], (DEPTH, 2, GLA_QK), 0.1),
        'gla_norm_g': gain(ks[13], (DEPTH, GLA_V)),
        'hg_lb_logits': nrm(ks[14], (2, DEPTH, HG_K), 0.1),
        'hg_norm_g': gain(ks[15], (DEPTH, HG_V)),
        'gm_norm_g': gain(ks[16], (DEPTH, GM_W)),
        'gm_ws': nrm(ks[17], (DEPTH, GM_G, GM_CHUNK, GM_CHUNK), GM_CHUNK ** -0.5),
        'gm_bs': gain(ks[18], (DEPTH, GM_G, GM_CHUNK)),
        'w_br_gla': nrm(ks[19], (DEPTH, GLA_V, D), GLA_V ** -0.5),
        'w_br_hg': nrm(ks[20], (DEPTH, HG_V, D), HG_V ** -0.5),
        'w_br_gm': nrm(ks[21], (DEPTH, GM_W, D), GM_W ** -0.5),
        'w_out': nrm(ks[22], (DEPTH, D, D), D ** -0.5),
        'w_ff1': nrm(ks[23], (DEPTH, D, D_FF), D ** -0.5),
        'w_ff2': nrm(ks[24], (DEPTH, D_FF, D), D_FF ** -0.5),
        'final_norm_g': gain(ks[25], (D,)),
    }


def reference(x_prompt, x_sample, c, state_gla, state_hgrn, c_ctx, ada_w, ada_b, norm_mix_g, norm_ffn_g,
              w_in, gla_lr_w, gla_lr_b, gla_norm_g, hg_lb_logits, hg_norm_g, gm_norm_g, gm_ws, gm_bs,
              w_br_gla, w_br_hg, w_br_gm, w_out, w_ff1, w_ff2, final_norm_g):
    lbs = hgrn_lower_bounds(hg_lb_logits)
    b_ctx = x_prompt.shape[0]
    xc = x_prompt
    xl = x_sample + grid_posemb(x_sample.shape[1], x_sample.dtype)[None]
    zero_gla = jnp.zeros((b_ctx, 2, GLA_H, GLA_DK, GLA_DV), x_prompt.dtype)
    zero_hg = jnp.zeros((b_ctx, 2, HG_H, HG_DK, HG_DV), x_prompt.dtype)
    new_gla = []
    new_hg = []
    for l in range(DEPTH):
        p = {
            'norm_mix_g': norm_mix_g[l], 'norm_ffn_g': norm_ffn_g[l], 'w_in': w_in[l],
            'gla_lr_w': gla_lr_w[l], 'gla_lr_b': gla_lr_b[l], 'gla_norm_g': gla_norm_g[l],
            'hg_norm_g': hg_norm_g[l], 'gm_norm_g': gm_norm_g[l], 'gm_ws': gm_ws[l], 'gm_bs': gm_bs[l],
            'w_br_gla': w_br_gla[l], 'w_br_hg': w_br_hg[l], 'w_br_gm': w_br_gm[l], 'w_out': w_out[l],
            'w_ff1': w_ff1[l], 'w_ff2': w_ff2[l],
        }
        mod_c = (jax.nn.silu(c_ctx) @ ada_w[l] + ada_b[l])[None, None, :]
        mod_l = (jax.nn.silu(c) @ ada_w[l] + ada_b[l])[:, None, :]
        xc, sg, sh = trunk_layer(xc, mod_c, zero_gla, zero_hg, lbs[0, l], lbs[1, l], p)
        new_gla.append(sg)
        new_hg.append(sh)
        xl, _, _ = trunk_layer(xl, mod_l, state_gla[:, l], state_hgrn[:, l], lbs[0, l], lbs[1, l], p)
    y_prompt = rms_norm(xc, final_norm_g)
    y_sample = rms_norm(xl, final_norm_g)
    new_state_gla = jnp.stack(new_gla, axis=1)
    new_state_hgrn = jnp.stack(new_hg, axis=1)
    return (y_prompt, y_sample, new_state_gla, new_state_hgrn)
```

```python
import functools

import jax
import jax.numpy as jnp
import numpy as np
from jax import lax
from jax.experimental import pallas as pl
from jax.experimental.pallas import tpu as pltpu

F32 = jnp.float32
BF16 = jnp.bfloat16

D_MODEL = 1024
GRID_W = 64
GLA_H, GLA_DK, GLA_DV, GLA_RANK = 4, 64, 128, 16
GLA_GATE_NORM = 16.0
HG_H, HG_DK, HG_DV = 4, 128, 128
GM_G, GM_CH, GM_CHUNK = 4, 128, 128
D_FF = 4 * D_MODEL
EPS = 1e-6
GLA_QK = GLA_H * GLA_DK
GLA_V = GLA_H * GLA_DV
HG_K = HG_H * HG_DK
HG_V = HG_H * HG_DV
GM_W = GM_G * GM_CH
GZ_START = 2 * GLA_QK + 2 * GLA_V
GZ_WIDTH = 2 * GLA_RANK
D_MAIN = 2 * GLA_QK + 2 * GLA_V + 3 * HG_K + 2 * HG_V + 2 * GM_W + 3 * D_MODEL

G_GQK, G_GV, G_GO, G_HQ, G_HFF, G_HFB, G_HI, G_HO = range(8)
G_MUV = 4
G_AGLA, G_AHG, G_AGM = 5, 6, 7

LANE = 128
SCAN_C = 128
GROUP_K = 256
LEAF = 32
ROW_TILE = 1024
MERGE_TILE = 256
VMEM_LIMIT = 56 * 1024 * 1024


def _cparams(sem):
    return pltpu.CompilerParams(dimension_semantics=sem, vmem_limit_bytes=VMEM_LIMIT)


def _dot(a, b):
    return jnp.dot(a, b, preferred_element_type=F32)


def _dot_nt(a, b):
    return lax.dot_general(a, b, (((1,), (1,)), ((), ())), preferred_element_type=F32)


def _dot_tn(a, b):
    return lax.dot_general(a, b, (((0,), (0,)), ((), ())), preferred_element_type=F32)


def _sigmoid_pair(x):
    e = jnp.exp(-jnp.abs(x))
    r = 1.0 / (1.0 + e)
    er = e * r
    pos = x >= 0
    return jnp.where(pos, r, er), jnp.where(pos, er, r)


def _sigmoid(x):
    return _sigmoid_pair(x)[0]


def _log_sigmoid(x):
    return jnp.minimum(x, 0.0) - jnp.log(1.0 + jnp.exp(-jnp.abs(x)))


def _gelu_tanh(x):
    c = np.sqrt(2.0 / np.pi).astype(np.float32)
    return x * (0.5 * (1.0 + jnp.tanh(c * (x + 0.044715 * (x * x * x)))))


def _embed_kernel(x_ref, pos_ref, o_ref):
    o_ref[...] = x_ref[...] + pos_ref[...]


def _embed(x_sample, pos):
    b, s, d = x_sample.shape
    ts = min(s, 512)
    return pl.pallas_call(
        _embed_kernel,
        grid=(b, s // ts),
        in_specs=[pl.BlockSpec((None, ts, d), lambda i, j: (i, j, 0)),
                  pl.BlockSpec((ts, d), lambda i, j: (j, 0))],
        out_specs=pl.BlockSpec((None, ts, d), lambda i, j: (i, j, 0)),
        out_shape=jax.ShapeDtypeStruct((b, s, d), F32),
        compiler_params=_cparams(("parallel", "parallel")),
        name="embed",
    )(x_sample, pos)


def _mod_kernel(c_ref, w_ref, b_ref, o_ref):
    c = c_ref[...]
    s = (c * _sigmoid(c)).astype(BF16)
    o_ref[...] = _dot(s, w_ref[...].astype(BF16)) + b_ref[...]


def _modulations(cvec, ada_w, ada_b):
    depth, d, d6 = ada_w.shape
    r = cvec.shape[0]
    tn = 1024
    return pl.pallas_call(
        _mod_kernel,
        grid=(depth, d6 // tn),
        in_specs=[pl.BlockSpec((r, d), lambda l, j: (0, 0)),
                  pl.BlockSpec((None, d, tn), lambda l, j: (l, 0, j)),
                  pl.BlockSpec((None, 1, tn), lambda l, j: (l, 0, j))],
        out_specs=pl.BlockSpec((None, r, tn), lambda l, j: (l, 0, j)),
        out_shape=jax.ShapeDtypeStruct((depth, r, d6), F32),
        compiler_params=_cparams(("parallel", "parallel")),
        name="adaln_mod",
    )(cvec, ada_w, ada_b.reshape(depth, 1, d6))


def _lb_kernel(x_ref, o_ref):
    x = x_ref[...]
    m = jnp.max(x, axis=1, keepdims=True)
    e = jnp.exp(x - m)
    p = e / jnp.sum(e, axis=1, keepdims=True)
    acc = jnp.zeros_like(p[:, 0:1, :])
    o_ref[:, 0:1, :] = acc
    for l in range(1, x.shape[1]):
        acc = acc + p[:, l:l + 1, :]
        o_ref[:, l:l + 1, :] = acc


def _lower_bounds(logits):
    return pl.pallas_call(
        _lb_kernel,
        out_shape=jax.ShapeDtypeStruct(logits.shape, F32),
        name="hgrn_lower_bounds",
    )(logits)


def _inproj_kernel(x_ref, mod_ref, g_ref, w_ref, wgz_ref, out_ref, gz_ref, h_ref):
    @pl.when(pl.program_id(1) == 0)
    def _():
        x = x_ref[...]
        ms = jnp.mean(x * x, axis=-1, keepdims=True)
        hn = (x * lax.rsqrt(ms + EPS)) * g_ref[...]
        h = (hn * (1.0 + mod_ref[1:2, :]) + mod_ref[0:1, :]).astype(BF16)
        h_ref[...] = h
        gz_ref[...] = _dot(h, wgz_ref[...])

    out_ref[...] = _dot(h_ref[...], w_ref[...])


def _inproj(x, mods, g, w_main, w_gz, l, mod_row, seq_rows):
    t, d = x.shape
    tm = min(ROW_TILE, seq_rows)
    tn = 1024
    per_seq = seq_rows // tm
    return pl.pallas_call(
        _inproj_kernel,
        grid=(t // tm, D_MAIN // tn),
        in_specs=[pl.BlockSpec((tm, d), lambda i, j: (i, 0)),
                  pl.BlockSpec((None, None, 6, d), lambda i, j: (l, mod_row(i // per_seq), 0, 0)),
                  pl.BlockSpec((None, 1, d), lambda i, j: (l, 0, 0)),
                  pl.BlockSpec((None, d, tn), lambda i, j: (l, 0, j)),
                  pl.BlockSpec((None, d, LANE), lambda i, j: (l, 0, 0))],
        out_specs=[pl.BlockSpec((tm, tn), lambda i, j: (i, j)),
                   pl.BlockSpec((tm, LANE), lambda i, j: (i, 0))],
        out_shape=[jax.ShapeDtypeStruct((t, D_MAIN), F32),
                   jax.ShapeDtypeStruct((t, LANE), F32)],
        scratch_shapes=[pltpu.VMEM((tm, d), BF16)],
        compiler_params=_cparams(("parallel", "arbitrary")),
        name="inproj",
    )(x, mods, g, w_main, w_gz)


def _exact_dot01(m01, x):
    x1 = x.astype(BF16)
    r1 = x - x1.astype(F32)
    x2 = r1.astype(BF16)
    r2 = r1 - x2.astype(F32)
    x3 = r2.astype(BF16)
    return _dot(m01, x1) + _dot(m01, x2) + _dot(m01, x3)


def _div_p2(x, n):
    return lax.shift_right_logical(x, int(n).bit_length() - 1)


def _mod_p2(x, n):
    return lax.bitwise_and(x, n - 1)


def _row_bcast_blocks(a, block, row):
    n = a.shape[0] // block
    parts = [jnp.broadcast_to(a[i * block + row:i * block + row + 1, :], (block, a.shape[1])) for i in range(n)]
    return parts[0] if n == 1 else jnp.concatenate(parts, axis=0)


def _chunk_core(q, k, la, v, st, *, fwd, hg, dv):
    c = q.shape[0]
    dk = GROUP_K // hg
    ri = lax.broadcasted_iota(jnp.int32, (c, c), 0)
    ci = lax.broadcasted_iota(jnp.int32, (c, c), 1)
    tri = ((ci <= ri) if fwd else (ci >= ri)).astype(BF16)
    cum = _exact_dot01(tri, la)
    total = cum[c - 1:c, :] if fwd else cum[0:1, :]

    q_t = (q * jnp.exp(cum)).astype(BF16)
    k_t = (k * jnp.exp(total - cum)).astype(BF16)
    vb = v.astype(BF16)

    o = _dot_nt(q_t, st.astype(BF16))

    kv = _dot_tn(vb, k_t)
    srow = _div_p2(lax.broadcasted_iota(jnp.int32, kv.shape, 0), dv)
    scol = _div_p2(lax.broadcasted_iota(jnp.int32, kv.shape, 1), dk)
    st_new = st * jnp.exp(total) + jnp.where(srow == scol, kv, 0.0)

    lane_head = _div_p2(lax.broadcasted_iota(jnp.int32, (c, GROUP_K), 1), dk)
    row_id = lax.broadcasted_iota(jnp.int32, (c, GROUP_K), 0)
    si = lax.broadcasted_iota(jnp.int32, (c, hg * c), 0)
    sj = _mod_p2(lax.broadcasted_iota(jnp.int32, (c, hg * c), 1), c)

    def stack_heads(b):
        return jnp.concatenate([jnp.where(lane_head == h, b, 0.0) for h in range(hg)], axis=0).astype(BF16)

    scores = jnp.zeros((c, hg * c), F32)
    blk = c // 2
    while blk >= LEAF:
        sb = 2 * blk
        ref = _row_bcast_blocks(cum, sb, blk - 1 if fwd else blk)
        second_half = _mod_p2(row_id, sb) >= blk
        a_rows = second_half if fwd else jnp.logical_not(second_half)
        e = jnp.exp(jnp.where(a_rows, cum - ref, ref - cum))
        p = _dot_nt((q * e).astype(BF16), stack_heads(k * e))
        i_second = _mod_p2(si, sb) >= blk
        j_second = _mod_p2(sj, sb) >= blk
        same = _div_p2(si, sb) == _div_p2(sj, sb)
        if fwd:
            m = same & i_second & jnp.logical_not(j_second)
        else:
            m = same & jnp.logical_not(i_second) & j_second
        scores = jnp.where(m, p, scores)
        blk //= 2
    ref = _row_bcast_blocks(cum, LEAF, LEAF // 2)
    d = cum - ref
    p = _dot_nt((q * jnp.exp(d)).astype(BF16), stack_heads(k * jnp.exp(-d)))
    m = (_div_p2(si, LEAF) == _div_p2(sj, LEAF)) & ((sj <= si) if fwd else (sj >= si))
    scores = jnp.where(m, p, scores)

    vlane_head = _div_p2(lax.broadcasted_iota(jnp.int32, (c, hg * dv), 1), dv)
    v_bd = jnp.concatenate([jnp.where(vlane_head == h, v, 0.0) for h in range(hg)], axis=0).astype(BF16)
    o = o + _dot(scores.astype(BF16), v_bd)
    return o, st_new


def _gla_scan_kernel(*refs, fwd, has_init, emit_final):
    qk_ref, v_ref, gz_ref, lrw_ref, lrb_ref = refs[:5]
    pos = 5
    s0_ref = None
    if has_init:
        s0_ref = refs[pos]
        pos += 1
    o_ref = refs[pos]
    pos += 1
    sfin_ref = None
    if emit_final:
        sfin_ref = refs[pos]
        pos += 1
    st_ref = refs[pos]

    ci = pl.program_id(1)

    @pl.when(ci == 0)
    def _():
        if has_init:
            st_ref[...] = s0_ref[...]
        else:
            st_ref[...] = jnp.zeros_like(st_ref)

    qk = qk_ref[...]
    q = qk[:, :GLA_QK] * (GLA_DK ** -0.5)
    k = qk[:, GLA_QK:]
    z = _dot(gz_ref[...].astype(BF16), lrw_ref[...]) + lrb_ref[...]
    la = _log_sigmoid(z) * (1.0 / GLA_GATE_NORM)
    o, st_new = _chunk_core(q, k, la, v_ref[...], st_ref[...], fwd=fwd, hg=GLA_H, dv=GLA_DV)
    o_ref[...] = o
    st_ref[...] = st_new
    if emit_final:
        @pl.when(ci == pl.num_programs(1) - 1)
        def _():
            sfin_ref[...] = st_new


def _gla_scan(proj, gz, lrw_pad, lrb, s0, *, l, d, n_seq, emit_final):
    t = proj.shape[0]
    nc = t // n_seq // SCAN_C
    fwd = d == 0
    has_init = s0 is not None

    def row(s, c):
        return s * nc + (c if fwd else nc - 1 - c)

    in_specs = [pl.BlockSpec((SCAN_C, 512), lambda s, c: (row(s, c), G_GQK)),
                pl.BlockSpec((SCAN_C, 512), lambda s, c: (row(s, c), G_GV)),
                pl.BlockSpec((SCAN_C, LANE), lambda s, c: (row(s, c), 0)),
                pl.BlockSpec((None, None, LANE, GLA_QK), lambda s, c: (l, d, 0, 0)),
                pl.BlockSpec((None, None, 1, GLA_QK), lambda s, c: (l, d, 0, 0))]
    args = [proj, proj, gz, lrw_pad, lrb]
    if has_init:
        in_specs.append(pl.BlockSpec((None, None, None, GLA_V, GROUP_K), lambda s, c: (s, l, d, 0, 0)))
        args.append(s0)
    out_specs = [pl.BlockSpec((SCAN_C, GLA_V), lambda s, c: (row(s, c), 0))]
    out_shape = [jax.ShapeDtypeStruct((t, GLA_V), F32)]
    if emit_final:
        out_specs.append(pl.BlockSpec((None, GLA_V, GROUP_K), lambda s, c: (s, 0, 0)))
        out_shape.append(jax.ShapeDtypeStruct((n_seq, GLA_V, GROUP_K), F32))
    return pl.pallas_call(
        functools.partial(_gla_scan_kernel, fwd=fwd, has_init=has_init, emit_final=emit_final),
        grid=(n_seq, nc),
        in_specs=in_specs,
        out_specs=out_specs,
        out_shape=out_shape,
        scratch_shapes=[pltpu.VMEM((GLA_V, GROUP_K), F32)],
        compiler_params=_cparams(("parallel", "arbitrary")),
        name="gla_scan_" + ("fwd" if fwd else "bwd"),
    )(*args)


HG_GROUPS = HG_K // GROUP_K
HG_GH = HG_H // HG_GROUPS
HG_GV = HG_GH * HG_DV


def _hg_scan_kernel(*refs, fwd, has_init, emit_final):
    q_ref, f_ref, v_ref, lb_ref = refs[:4]
    pos = 4
    s0_ref = None
    if has_init:
        s0_ref = refs[pos]
        pos += 1
    o_ref = refs[pos]
    pos += 1
    sfin_ref = None
    if emit_final:
        sfin_ref = refs[pos]
        pos += 1
    st_ref = refs[pos]

    ci = pl.program_id(1)

    @pl.when(ci == 0)
    def _():
        if has_init:
            st_ref[...] = s0_ref[...]
        else:
            st_ref[...] = jnp.zeros_like(st_ref)

    for g in range(HG_GROUPS):
        ks = slice(g * GROUP_K, (g + 1) * GROUP_K)
        vs = slice(g * HG_GV, (g + 1) * HG_GV)
        q = q_ref[:, ks] * (HG_DK ** -0.5)
        lb = lb_ref[:, ks]
        sig, nsig = _sigmoid_pair(f_ref[:, ks])
        la = jnp.log(lb + (1.0 - lb) * sig)
        k = (1.0 - lb) * nsig
        o, st_new = _chunk_core(q, k, la, v_ref[:, vs], st_ref[g], fwd=fwd, hg=HG_GH, dv=HG_DV)
        o_ref[:, vs] = o
        st_ref[g] = st_new
        if emit_final:
            @pl.when(ci == pl.num_programs(1) - 1)
            def _(g=g, st_new=st_new):
                sfin_ref[g] = st_new


def _hg_scan(proj, lbs, s0, *, l, d, n_seq, emit_final):
    t = proj.shape[0]
    nc = t // n_seq // SCAN_C
    fwd = d == 0
    has_init = s0 is not None

    def row(s, c):
        return s * nc + (c if fwd else nc - 1 - c)

    in_specs = [pl.BlockSpec((SCAN_C, 512), lambda s, c: (row(s, c), G_HQ)),
                pl.BlockSpec((SCAN_C, 512), lambda s, c: (row(s, c), G_HFF if fwd else G_HFB)),
                pl.BlockSpec((SCAN_C, 512), lambda s, c: (row(s, c), G_HI)),
                pl.BlockSpec((None, None, 1, HG_K), lambda s, c: (d, l, 0, 0))]
    args = [proj, proj, proj, lbs]
    if has_init:
        in_specs.append(pl.BlockSpec((None, None, None, HG_GROUPS, HG_GV, GROUP_K), lambda s, c: (s, l, d, 0, 0, 0)))
        args.append(s0)
    out_specs = [pl.BlockSpec((SCAN_C, HG_V), lambda s, c: (row(s, c), 0))]
    out_shape = [jax.ShapeDtypeStruct((t, HG_V), F32)]
    if emit_final:
        out_specs.append(pl.BlockSpec((None, HG_GROUPS, HG_GV, GROUP_K), lambda s, c: (s, 0, 0, 0)))
        out_shape.append(jax.ShapeDtypeStruct((n_seq, HG_GROUPS, HG_GV, GROUP_K), F32))
    return pl.pallas_call(
        functools.partial(_hg_scan_kernel, fwd=fwd, has_init=has_init, emit_final=emit_final),
        grid=(n_seq, nc),
        in_specs=in_specs,
        out_specs=out_specs,
        out_shape=out_shape,
        scratch_shapes=[pltpu.VMEM((HG_GROUPS, HG_GV, GROUP_K), F32)],
        compiler_params=_cparams(("parallel", "arbitrary")),
        name="hgrn_scan_" + ("fwd" if fwd else "bwd"),
    )(*args)


def _head_norm_gate(o, g, gate, n_heads, dv):
    parts = []
    for h in range(n_heads):
        oh = o[:, h * dv:(h + 1) * dv]
        parts.append(oh * lax.rsqrt(jnp.mean(oh * oh, axis=-1, keepdims=True) + EPS))
    of = jnp.concatenate(parts, axis=1)
    return of * g * (gate * _sigmoid(gate))


def _merge_kernel(x_ref, ogf_ref, ogb_ref, ohf_ref, ohb_ref, go_ref, ho_ref, muv_ref, agla_ref, ahg_ref, agm_ref,
                  mod_ref, gng_ref, hng_ref, gmg_ref, ws_ref, bias_ref, wbg_ref, wbh_ref, wbm_ref, wout_ref,
                  out_ref):
    o_gla = _head_norm_gate(ogf_ref[...] + ogb_ref[...], gng_ref[...], go_ref[...], GLA_H, GLA_DV)
    o_hg = _head_norm_gate(ohf_ref[...] + ohb_ref[...], hng_ref[...], ho_ref[...], HG_H, HG_DV)

    u = _gelu_tanh(muv_ref[:, :GM_W])
    v = _gelu_tanh(muv_ref[:, GM_W:])
    mu = jnp.mean(v, axis=-1, keepdims=True)
    vc = v - mu
    var = jnp.mean(vc * vc, axis=-1, keepdims=True)
    vn = ((vc * lax.rsqrt(var + EPS)) * gmg_ref[...]).astype(BF16)
    rows = []
    for n in range(vn.shape[0] // GM_CHUNK):
        cols = [_dot(ws_ref[g], vn[n * GM_CHUNK:(n + 1) * GM_CHUNK, g * GM_CH:(g + 1) * GM_CH]) for g in range(GM_G)]
        rows.append(jnp.concatenate(cols, axis=1) + bias_ref[...])
    mixed = rows[0] if len(rows) == 1 else jnp.concatenate(rows, axis=0)
    o_gm = u * mixed

    merged = (_sigmoid(agla_ref[...]) * _dot(o_gla.astype(BF16), wbg_ref[...])
              + _sigmoid(ahg_ref[...]) * _dot(o_hg.astype(BF16), wbh_ref[...])
              + _sigmoid(agm_ref[...]) * _dot(o_gm.astype(BF16), wbm_ref[...]))
    mix = _dot(merged.astype(BF16), wout_ref[...])
    out_ref[...] = x_ref[...] + mod_ref[2:3, :] * mix


def _merge(x, og_f, og_b, oh_f, oh_b, proj, mods, p, l, mod_row, seq_rows):
    t, d = x.shape
    tm = min(MERGE_TILE, seq_rows)
    per_seq = seq_rows // tm
    row_spec = lambda w, col: pl.BlockSpec((tm, w), lambda i: (i, col))
    lay = lambda *shape: pl.BlockSpec((None,) + shape, lambda i: (l,) + (0,) * len(shape))
    return pl.pallas_call(
        _merge_kernel,
        grid=(t // tm,),
        in_specs=[row_spec(d, 0),
                  row_spec(GLA_V, 0), row_spec(GLA_V, 0), row_spec(HG_V, 0), row_spec(HG_V, 0),
                  row_spec(512, G_GO), row_spec(512, G_HO), row_spec(1024, G_MUV),
                  row_spec(1024, G_AGLA), row_spec(1024, G_AHG), row_spec(1024, G_AGM),
                  pl.BlockSpec((None, None, 6, d), lambda i: (l, mod_row(i // per_seq), 0, 0)),
                  lay(1, GLA_V), lay(1, HG_V), lay(1, GM_W),
                  lay(GM_G, GM_CHUNK, GM_CHUNK), lay(GM_CHUNK, GM_W),
                  lay(GLA_V, d), lay(HG_V, d), lay(GM_W, d), lay(d, d)],
        out_specs=pl.BlockSpec((tm, d), lambda i: (i, 0)),
        out_shape=jax.ShapeDtypeStruct((t, d), F32),
        compiler_params=_cparams(("parallel",)),
        name="merge",
    )(x, og_f, og_b, oh_f, oh_b, proj, proj, proj, proj, proj, proj, mods,
      p["gla_norm_g"], p["hg_norm_g"], p["gm_norm_g"], p["gm_ws"], p["gm_bias"],
      p["w_br_gla"], p["w_br_hg"], p["w_br_gm"], p["w_out"])


def _ffn_kernel(x_ref, mod_ref, g_ref, w1_ref, w2_ref, out_ref, h_ref, acc_ref):
    j = pl.program_id(1)

    @pl.when(j == 0)
    def _():
        x = x_ref[...]
        ms = jnp.mean(x * x, axis=-1, keepdims=True)
        hn = (x * lax.rsqrt(ms + EPS)) * g_ref[...]
        h_ref[...] = (hn * (1.0 + mod_ref[4:5, :]) + mod_ref[3:4, :]).astype(BF16)
        acc_ref[...] = jnp.zeros_like(acc_ref)

    a = jnp.maximum(_dot(h_ref[...], w1_ref[...]), 0.0)
    acc_ref[...] += _dot((a * a).astype(BF16), w2_ref[...])

    @pl.when(j == pl.num_programs(1) - 1)
    def _():
        out_ref[...] = x_ref[...] + mod_ref[5:6, :] * acc_ref[...]


def _ffn(x, mods, g, w1, w2, l, mod_row, seq_rows):
    t, d = x.shape
    tm = min(ROW_TILE, seq_rows)
    per_seq = seq_rows // tm
    tf = 1024
    return pl.pallas_call(
        _ffn_kernel,
        grid=(t // tm, D_FF // tf),
        in_specs=[pl.BlockSpec((tm, d), lambda i, j: (i, 0)),
                  pl.BlockSpec((None, None, 6, d), lambda i, j: (l, mod_row(i // per_seq), 0, 0)),
                  pl.BlockSpec((None, 1, d), lambda i, j: (l, 0, 0)),
                  pl.BlockSpec((None, d, tf), lambda i, j: (l, 0, j)),
                  pl.BlockSpec((None, tf, d), lambda i, j: (l, j, 0))],
        out_specs=pl.BlockSpec((tm, d), lambda i, j: (i, 0)),
        out_shape=jax.ShapeDtypeStruct((t, d), F32),
        scratch_shapes=[pltpu.VMEM((tm, d), BF16), pltpu.VMEM((tm, d), F32)],
        compiler_params=_cparams(("parallel", "arbitrary")),
        name="ffn",
    )(x, mods, g, w1, w2)


def _final_norm_kernel(x_ref, g_ref, o_ref):
    x = x_ref[...]
    ms = jnp.mean(x * x, axis=-1, keepdims=True)
    o_ref[...] = (x * lax.rsqrt(ms + EPS)) * g_ref[...]


def _final_norm(x, g):
    t, d = x.shape
    tm = min(ROW_TILE, t)
    return pl.pallas_call(
        _final_norm_kernel,
        grid=(t // tm,),
        in_specs=[pl.BlockSpec((tm, d), lambda i: (i, 0)), pl.BlockSpec((1, d), lambda i: (0, 0))],
        out_specs=pl.BlockSpec((tm, d), lambda i: (i, 0)),
        out_shape=jax.ShapeDtypeStruct((t, d), F32),
        compiler_params=_cparams(("parallel",)),
        name="final_norm",
    )(x, g.reshape(1, d))


def _grid_posemb(n_tok):
    rows = n_tok // GRID_W
    r, col = jnp.meshgrid(jnp.arange(rows, dtype=F32), jnp.arange(GRID_W, dtype=F32), indexing="ij")
    r = r.reshape(-1)
    col = col.reshape(-1)
    nf = D_MODEL // 4
    omega = 1.0 / (10000.0 ** (jnp.arange(nf, dtype=F32) / nf))
    er = r[:, None] * omega
    ec = col[:, None] * omega
    return jnp.concatenate([jnp.sin(er), jnp.cos(er), jnp.sin(ec), jnp.cos(ec)], axis=-1)


def _state_to_blockdiag(s, gh):
    *lead, h, dk, dv = s.shape
    st = jnp.swapaxes(s, -1, -2).reshape(*lead, h // gh, gh, dv, dk)
    eye = jnp.eye(gh, dtype=s.dtype)
    bd = st[..., :, :, :, None, :] * eye[:, None, :, None]
    return bd.reshape(*lead, h // gh, gh * dv, gh * dk)


def _blockdiag_to_state(bd, gh, dk, dv):
    *lead, g, _, _ = bd.shape
    x = bd.reshape(*lead, g, gh, dv, gh, dk)
    diag = jnp.stack([x[..., :, i, :, i, :] for i in range(gh)], axis=-3)
    return jnp.swapaxes(diag, -1, -2).reshape(*lead, g * gh, dk, dv)


def kernel(x_prompt, x_sample, c, state_gla, state_hgrn, c_ctx, ada_w, ada_b, norm_mix_g, norm_ffn_g, w_in, gla_lr_w,
           gla_lr_b, gla_norm_g, hg_lb_logits, hg_norm_g, gm_norm_g, gm_ws, gm_bs, w_br_gla, w_br_hg, w_br_gm, w_out,
           w_ff1, w_ff2, final_norm_g):
    batch, seq, d = x_prompt.shape
    dec_batch, dec_seq, _ = x_sample.shape
    depth = ada_w.shape[0]

    w_main = jnp.concatenate([w_in[..., :GZ_START], w_in[..., GZ_START + GZ_WIDTH:]], axis=-1).astype(BF16)
    w_gz = jnp.pad(w_in[..., GZ_START:GZ_START + GZ_WIDTH], ((0, 0), (0, 0), (0, LANE - GZ_WIDTH))).astype(BF16)
    lrw_pad = jnp.stack(
        [jnp.pad(gla_lr_w[:, dd], ((0, 0), (dd * GLA_RANK, LANE - (dd + 1) * GLA_RANK), (0, 0))) for dd in range(2)],
        axis=1).astype(BF16)
    lrb = gla_lr_b.reshape(depth, 2, 1, GLA_QK)
    gm_bias = jnp.broadcast_to(jnp.swapaxes(gm_bs, 1, 2)[:, :, :, None],
                               (depth, GM_CHUNK, GM_G, GM_CH)).reshape(depth, GM_CHUNK, GM_W)
    params = {
        "gla_norm_g": gla_norm_g.reshape(depth, 1, GLA_V),
        "hg_norm_g": hg_norm_g.reshape(depth, 1, HG_V),
        "gm_norm_g": gm_norm_g.reshape(depth, 1, GM_W),
        "gm_ws": gm_ws.astype(BF16),
        "gm_bias": gm_bias,
        "w_br_gla": w_br_gla.astype(BF16),
        "w_br_hg": w_br_hg.astype(BF16),
        "w_br_gm": w_br_gm.astype(BF16),
        "w_out": w_out.astype(BF16),
    }
    w1 = w_ff1.astype(BF16)
    w2 = w_ff2.astype(BF16)
    g_mix = norm_mix_g.reshape(depth, 1, d)
    g_ffn = norm_ffn_g.reshape(depth, 1, d)

    s0_gla = _state_to_blockdiag(state_gla, GLA_H)[..., 0, :, :]
    s0_hg = _state_to_blockdiag(state_hgrn, HG_GH)

    n_mod = 1 + dec_batch
    n_mod_pad = -(-n_mod // 8) * 8
    cvec = jnp.concatenate([c_ctx[None, :], c, jnp.zeros((n_mod_pad - n_mod, d), F32)], axis=0)
    mods = _modulations(cvec, ada_w, ada_b).reshape(depth, n_mod_pad, 6, d)
    lbs = _lower_bounds(hg_lb_logits).reshape(2, depth, 1, HG_K)

    xc = x_prompt.reshape(batch * seq, d)
    xl = _embed(x_sample, _grid_posemb(dec_seq)).reshape(dec_batch * dec_seq, d)

    ctx_row = lambda s: 0
    lat_row = lambda s: 1 + s
    new_gla = []
    new_hg = []
    for l in range(depth):
        proj, gz = _inproj(xc, mods, g_mix, w_main, w_gz, l, ctx_row, seq)
        og, sg, oh, sh = [], [], [], []
        for dd in range(2):
            o, s = _gla_scan(proj, gz, lrw_pad, lrb, None, l=l, d=dd, n_seq=batch, emit_final=True)
            og.append(o)
            sg.append(s)
            o, s = _hg_scan(proj, lbs, None, l=l, d=dd, n_seq=batch, emit_final=True)
            oh.append(o)
            sh.append(s)
        new_gla.append(jnp.stack(sg, axis=1))
        new_hg.append(jnp.stack(sh, axis=1))
        xc = _merge(xc, og[0], og[1], oh[0], oh[1], proj, mods, params, l, ctx_row, seq)
        xc = _ffn(xc, mods, g_ffn, w1, w2, l, ctx_row, seq)

        proj, gz = _inproj(xl, mods, g_mix, w_main, w_gz, l, lat_row, dec_seq)
        og, oh = [], []
        for dd in range(2):
            og.append(_gla_scan(proj, gz, lrw_pad, lrb, s0_gla, l=l, d=dd, n_seq=dec_batch, emit_final=False)[0])
            oh.append(_hg_scan(proj, lbs, s0_hg, l=l, d=dd, n_seq=dec_batch, emit_final=False)[0])
        xl = _merge(xl, og[0], og[1], oh[0], oh[1], proj, mods, params, l, lat_row, dec_seq)
        xl = _ffn(xl, mods, g_ffn, w1, w2, l, lat_row, dec_seq)

    y_prompt = _final_norm(xc, final_norm_g).reshape(batch, seq, d)
    y_sample = _final_norm(xl, final_norm_g).reshape(dec_batch, dec_seq, d)
    ng = jnp.stack(new_gla, axis=1)[:, :, :, None]
    new_state_gla = _blockdiag_to_state(ng, GLA_H, GLA_DK, GLA_DV)
    new_state_hgrn = _blockdiag_to_state(jnp.stack(new_hg, axis=1), HG_GH, HG_DK, HG_DV)
    return (y_prompt, y_sample, new_state_gla, new_state_hgrn)
```

```python
import functools

import jax
import jax.numpy as jnp
import numpy as np
from jax import lax
from jax.experimental import pallas as pl
from jax.experimental.pallas import tpu as pltpu

F32 = jnp.float32
BF16 = jnp.bfloat16

D_MODEL = 1024
GRID_W = 64
GLA_H, GLA_DK, GLA_DV, GLA_RANK = 4, 64, 128, 16
GLA_GATE_NORM = 16.0
HG_H, HG_DK, HG_DV = 4, 128, 128
GM_G, GM_CH, GM_CHUNK = 4, 128, 128
D_FF = 4 * D_MODEL
EPS = 1e-6
GLA_QK = GLA_H * GLA_DK
GLA_V = GLA_H * GLA_DV
HG_K = HG_H * HG_DK
HG_V = HG_H * HG_DV
GM_W = GM_G * GM_CH
SPLIT_SIZES = (GLA_QK, GLA_QK, GLA_V, GLA_V, GLA_RANK, GLA_RANK, HG_K, HG_K, HG_K, HG_V, HG_V, GM_W, GM_W,
               D_MODEL, D_MODEL, D_MODEL)
SPLIT_NAMES = ("gq", "gk", "gv", "go", "gzf", "gzb", "hq", "hff", "hfb", "hi", "ho", "mu", "mv",
               "a_gla", "a_hg", "a_gm")
MAIN_COLS = ("gq", "gk", "gv", "go", "hq", "hi", "ho", "mu", "mv", "a_gla", "a_hg", "a_gm")
GATE_COLS = ("hff", "hfb", "gzf", "gzb")
D_MAIN = 7168
D_GATE = 2 * HG_K + 128

G_GQK, G_GV, G_GO, G_HQ, G_HI, G_HO = range(6)
G_MUV = 3
G_AGLA, G_AHG, G_AGM = 4, 5, 6
G_HFF, G_HFB = 0, 1
G_GZ = 2 * HG_K // 128

LANE = 128
SCAN_C = 128
GROUP_K = 256
SUB = 16
N_SUB = SCAN_C // SUB
LEVELS = (4, 2, 1)
ROW_TILE = 1024
MERGE_TILE = 256
VMEM_LIMIT = 56 * 1024 * 1024

HG_GROUPS = HG_K // GROUP_K
HG_GH = HG_H // HG_GROUPS
HG_GV = HG_GH * HG_DV


def _cparams(sem):
    return pltpu.CompilerParams(dimension_semantics=sem, vmem_limit_bytes=VMEM_LIMIT)


def _dot(a, b):
    return jnp.dot(a, b, preferred_element_type=F32)


def _dot_nt(a, b):
    return lax.dot_general(a, b, (((1,), (1,)), ((), ())), preferred_element_type=F32)


def _dot_tn(a, b):
    return lax.dot_general(a, b, (((0,), (0,)), ((), ())), preferred_element_type=F32)


def _sigmoid(x):
    return 0.5 + 0.5 * jnp.tanh(0.5 * x)


def _log_sigmoid(x):
    return jnp.minimum(x, 0.0) - jnp.log(1.0 + jnp.exp(-jnp.abs(x)))


def _gelu_tanh(x):
    c = np.sqrt(2.0 / np.pi).astype(np.float32)
    return x * (0.5 * (1.0 + jnp.tanh(c * (x + 0.044715 * (x * x * x)))))


def _embed_kernel(x_ref, pos_ref, o_ref):
    o_ref[...] = x_ref[...] + pos_ref[...]


def _embed(x_sample, pos):
    b, s, d = x_sample.shape
    ts = min(s, 512)
    return pl.pallas_call(
        _embed_kernel,
        grid=(b, s // ts),
        in_specs=[pl.BlockSpec((None, ts, d), lambda i, j: (i, j, 0)),
                  pl.BlockSpec((ts, d), lambda i, j: (j, 0))],
        out_specs=pl.BlockSpec((None, ts, d), lambda i, j: (i, j, 0)),
        out_shape=jax.ShapeDtypeStruct((b, s, d), F32),
        compiler_params=_cparams(("parallel", "parallel")),
        name="embed",
    )(x_sample, pos)


def _mod_kernel(c_ref, w_ref, b_ref, o_ref):
    c = c_ref[...]
    s = (c * _sigmoid(c)).astype(BF16)
    o_ref[...] = _dot(s, w_ref[...].astype(BF16)) + b_ref[...]


def _modulations(cvec, ada_w, ada_b):
    depth, d, d6 = ada_w.shape
    r = cvec.shape[0]
    tn = 1024
    return pl.pallas_call(
        _mod_kernel,
        grid=(depth, d6 // tn),
        in_specs=[pl.BlockSpec((r, d), lambda l, j: (0, 0)),
                  pl.BlockSpec((None, d, tn), lambda l, j: (l, 0, j)),
                  pl.BlockSpec((None, 1, tn), lambda l, j: (l, 0, j))],
        out_specs=pl.BlockSpec((None, r, tn), lambda l, j: (l, 0, j)),
        out_shape=jax.ShapeDtypeStruct((depth, r, d6), F32),
        compiler_params=_cparams(("parallel", "parallel")),
        name="adaln_mod",
    )(cvec, ada_w, ada_b.reshape(depth, 1, d6))


def _lb_kernel(x_ref, o_ref):
    x = x_ref[...]
    m = jnp.max(x, axis=1, keepdims=True)
    e = jnp.exp(x - m)
    p = e / jnp.sum(e, axis=1, keepdims=True)
    acc = jnp.zeros_like(p[:, 0:1, :])
    o_ref[:, 0:1, :] = acc
    for l in range(1, x.shape[1]):
        acc = acc + p[:, l:l + 1, :]
        o_ref[:, l:l + 1, :] = acc


def _lower_bounds(logits):
    return pl.pallas_call(
        _lb_kernel,
        out_shape=jax.ShapeDtypeStruct(logits.shape, F32),
        name="hgrn_lower_bounds",
    )(logits)


def _inproj_kernel(x_ref, mod_ref, g_ref, w_ref, wgate_ref, out_ref, gate_ref, h_ref):
    @pl.when(pl.program_id(1) == 0)
    def _():
        x = x_ref[...]
        ms = jnp.mean(x * x, axis=-1, keepdims=True)
        hn = (x * lax.rsqrt(ms + EPS)) * g_ref[...]
        h = (hn * (1.0 + mod_ref[1:2, :]) + mod_ref[0:1, :]).astype(BF16)
        h_ref[...] = h
        gate_ref[...] = _dot(h, wgate_ref[...])

    out_ref[...] = _dot(h_ref[...], w_ref[...]).astype(out_ref.dtype)


def _inproj(x, mods, g, w_main, w_gate, l, mod_row, tm):
    t, d = x.shape
    tn = 1024
    return pl.pallas_call(
        _inproj_kernel,
        grid=(t // tm, D_MAIN // tn),
        in_specs=[pl.BlockSpec((tm, d), lambda i, j: (i, 0)),
                  pl.BlockSpec((None, None, 6, d), lambda i, j: (l, mod_row(i), 0, 0)),
                  pl.BlockSpec((None, 1, d), lambda i, j: (l, 0, 0)),
                  pl.BlockSpec((None, d, tn), lambda i, j: (l, 0, j)),
                  pl.BlockSpec((None, d, D_GATE), lambda i, j: (l, 0, 0))],
        out_specs=[pl.BlockSpec((tm, tn), lambda i, j: (i, j)),
                   pl.BlockSpec((tm, D_GATE), lambda i, j: (i, 0))],
        out_shape=[jax.ShapeDtypeStruct((t, D_MAIN), BF16),
                   jax.ShapeDtypeStruct((t, D_GATE), F32)],
        scratch_shapes=[pltpu.VMEM((tm, d), BF16)],
        compiler_params=_cparams(("parallel", "arbitrary")),
        name="inproj",
    )(x, mods, g, w_main, w_gate)


R_LOCAL = 0
R_BLOCK = SCAN_C
R_PRE = R_BLOCK + N_SUB
R_POST = R_PRE + N_SUB
R_LEVEL = R_POST + N_SUB
R_TOTAL = R_LEVEL + len(LEVELS) * N_SUB
R_ROWS = -(-(R_TOTAL + N_SUB) // 16) * 16


def _scan_matrices():
    pos = np.arange(SCAN_C)
    blk = pos // SUB
    out = np.zeros((2, R_ROWS, SCAN_C), np.float32)
    for d, fwd in enumerate((True, False)):
        order = np.arange(N_SUB) if fwd else N_SUB - 1 - np.arange(N_SUB)
        earlier_eq = (pos[None, :] <= pos[:, None]) if fwd else (pos[None, :] >= pos[:, None])
        out[d, R_LOCAL:R_LOCAL + SCAN_C] = (blk[:, None] == blk[None, :]) & earlier_eq
        um = order[:, None]
        un = order[None, :]

        def put(row0, cond):
            out[d, row0:row0 + N_SUB] = cond[:, blk]

        put(R_BLOCK, um == un)
        put(R_PRE, un < um)
        put(R_POST, un > um)
        for i, bb in enumerate(LEVELS):
            sb = 2 * bb
            w, wn = um % sb, un % sb
            q_side = w >= bb
            inside = np.where(q_side, (wn >= bb) & (wn < w), (wn > w) & (wn < bb))
            put(R_LEVEL + i * N_SUB, ((um // sb) == (un // sb)) & inside)
        put(R_TOTAL, np.ones((N_SUB, N_SUB), bool))
    return jnp.asarray(out, BF16)


def _per_block(x8):
    return jnp.concatenate([jnp.broadcast_to(x8[m:m + 1, :], (SUB, x8.shape[1])) for m in range(N_SUB)], axis=0)


def _div_p2(x, n):
    return lax.shift_right_logical(x, int(n).bit_length() - 1)


def _mod_p2(x, n):
    return lax.bitwise_and(x, n - 1)


class _Problem:
    def __init__(self, fwd, hg, dv, q, k, la, v, s_ref, s_idx, o_ref, o_cols):
        self.fwd, self.hg, self.dv = fwd, hg, dv
        self.q, self.k, self.la, self.v = q, k, la, v
        self.s_ref, self.s_idx, self.o_ref, self.o_cols = s_ref, s_idx, o_ref, o_cols


def _stack_heads(b, hg):
    c = b.shape[0]
    dk = GROUP_K // hg
    per_lane = LANE // dk
    zb = jnp.zeros((c, LANE), BF16)
    blocks = []
    for h in range(hg):
        lv = (h * dk) // LANE
        piece = b[:, lv * LANE:(lv + 1) * LANE]
        if per_lane > 1:
            sub_head = _div_p2(lax.broadcasted_iota(jnp.int32, (c, LANE), 1), dk)
            piece = jnp.where(sub_head == (h % per_lane), piece, 0.0)
        piece = piece.astype(BF16)
        blocks.append(jnp.concatenate([piece if j == lv else zb for j in range(GROUP_K // LANE)], axis=1))
    return jnp.concatenate(blocks, axis=0)


def _stage_factors(p, r):
    c = SCAN_C
    local = r[R_LOCAL:R_LOCAL + c]
    block_total = _per_block(r[R_BLOCK:R_BLOCK + N_SUB])
    small = jnp.exp(r[R_PRE:R_TOTAL])
    pre = _per_block(small[0:N_SUB])
    post = _per_block(small[N_SUB:2 * N_SUB])
    ql = p.q * jnp.exp(local)
    kr = p.k * jnp.exp(block_total - local)
    ki = p.k * jnp.exp(-local)
    p.q_t = (ql * pre).astype(BF16)
    p.k_t = (kr * post).astype(BF16)
    p.a_ops = [ql.astype(BF16)]
    p.b_ops = [_stack_heads(ki, p.hg)]
    for i in range(len(LEVELS)):
        f = _per_block(small[(2 + i) * N_SUB:(3 + i) * N_SUB])
        p.a_ops.append((ql * f).astype(BF16))
        p.b_ops.append(_stack_heads(kr * f, p.hg))
    tot_rows = jnp.concatenate([r[R_TOTAL:R_TOTAL + N_SUB]] * (GROUP_K // N_SUB), axis=0)
    eye = (lax.broadcasted_iota(jnp.int32, (GROUP_K, GROUP_K), 0)
           == lax.broadcasted_iota(jnp.int32, (GROUP_K, GROUP_K), 1))
    p.decay_col = jnp.exp(jnp.sum(jnp.where(eye, tot_rows, 0.0), axis=1, keepdims=True))


def _stage_products(p):
    p.s_in = p.s_ref[p.s_idx]
    p.o_inter = _dot(p.q_t, p.s_in.astype(BF16))
    p.kv = _dot_tn(p.k_t, p.v)
    p.prods = [_dot_nt(a, b) for a, b in zip(p.a_ops, p.b_ops)]


def _stage_finish(p):
    c, hg, dv = SCAN_C, p.hg, p.dv
    dk = GROUP_K // hg
    si = lax.broadcasted_iota(jnp.int32, (c, hg * c), 0)
    sj = _mod_p2(lax.broadcasted_iota(jnp.int32, (c, hg * c), 1), c)
    m = (_div_p2(si, SUB) == _div_p2(sj, SUB)) & ((sj <= si) if p.fwd else (sj >= si))
    scores = jnp.where(m, p.prods[0], 0.0)
    for i, bb in enumerate(LEVELS):
        blk = bb * SUB
        sb = 2 * blk
        i_second = _mod_p2(si, sb) >= blk
        j_second = _mod_p2(sj, sb) >= blk
        same = _div_p2(si, sb) == _div_p2(sj, sb)
        if p.fwd:
            m = same & i_second & jnp.logical_not(j_second)
        else:
            m = same & jnp.logical_not(i_second) & j_second
        scores = jnp.where(m, p.prods[1 + i], scores)

    zv = jnp.zeros((c, dv), BF16)
    v_bd = jnp.concatenate(
        [jnp.concatenate([p.v[:, h * dv:(h + 1) * dv] if j == h else zv for j in range(hg)], axis=1)
         for h in range(hg)], axis=0)
    o = p.o_inter + _dot(scores.astype(BF16), v_bd)
    p.o_ref[:, p.o_cols] = o.astype(p.o_ref.dtype)

    kv_bd = _blockdiag([p.kv[h * dk:(h + 1) * dk, h * dv:(h + 1) * dv] for h in range(hg)])
    p.s_out = p.s_in * p.decay_col + kv_bd
    p.s_ref[p.s_idx] = p.s_out


def _blockdiag(blocks):
    hg = len(blocks)
    z = jnp.zeros_like(blocks[0])
    return jnp.concatenate(
        [jnp.concatenate([blocks[h] if j == h else z for j in range(hg)], axis=1) for h in range(hg)], axis=0)


def _scan_kernel(*refs, has_init, emit_final):
    tiles = (refs[0:6], refs[6:12])
    mat_ref, lrw_ref, lrb_ref, lb_ref = refs[12:16]
    pos = 16
    if has_init:
        sg0_ref, sh0_ref = refs[pos:pos + 2]
        pos += 2
    og_refs = refs[pos:pos + 2]
    oh_refs = refs[pos + 2:pos + 4]
    pos += 4
    if emit_final:
        ng_ref, nh_ref = refs[pos:pos + 2]
        pos += 2
    sg_ref, sh_ref = refs[pos:pos + 2]

    ci = pl.program_id(1)

    @pl.when(ci == 0)
    def _():
        if has_init:
            for d in range(2):
                sg_ref[d] = _blockdiag([sg0_ref[d, h] for h in range(GLA_H)])
                for g in range(HG_GROUPS):
                    sh_ref[d, g] = _blockdiag([sh0_ref[d, g * HG_GH + h] for h in range(HG_GH)])
        else:
            sg_ref[...] = jnp.zeros_like(sg_ref)
            sh_ref[...] = jnp.zeros_like(sh_ref)

    probs = []
    for d in range(2):
        fwd = d == 0
        gqk_ref, gv_ref, gz_ref, hq_ref, hf_ref, hi_ref = tiles[d]
        z = _dot(gz_ref[...].astype(BF16), lrw_ref[d]) + lrb_ref[d]
        group = [_Problem(fwd, GLA_H, GLA_DV,
                          gqk_ref[:, :GLA_QK].astype(F32) * (GLA_DK ** -0.5), gqk_ref[:, GLA_QK:].astype(F32),
                          _log_sigmoid(z) * (1.0 / GLA_GATE_NORM), gv_ref[...],
                          sg_ref, (d,), og_refs[d], slice(0, GLA_V))]
        for g in range(HG_GROUPS):
            ks = slice(g * GROUP_K, (g + 1) * GROUP_K)
            vs = slice(g * HG_GV, (g + 1) * HG_GV)
            lb = lb_ref[d, :, ks]
            sig = 1.0 / (1.0 + jnp.exp(-hf_ref[:, ks]))
            group.append(_Problem(fwd, HG_GH, HG_DV,
                                  hq_ref[:, ks].astype(F32) * (HG_DK ** -0.5), (1.0 - lb) * (1.0 - sig),
                                  jnp.log(lb + (1.0 - lb) * sig), hi_ref[:, vs],
                                  sh_ref, (d, g), oh_refs[d], vs))
        la = jnp.concatenate([p.la for p in group], axis=1)
        la_hi = la.astype(BF16)
        la_lo = (la - la_hi.astype(F32)).astype(BF16)
        r = _dot(mat_ref[d], la_hi) + _dot(mat_ref[d], la_lo)
        for i, p in enumerate(group):
            p.r = r[:, i * GROUP_K:(i + 1) * GROUP_K]
        probs += group

    for p in probs:
        _stage_factors(p, p.r)
    for p in probs:
        _stage_products(p)
    for p in probs:
        _stage_finish(p)

    if emit_final:
        @pl.when(ci == pl.num_programs(1) - 1)
        def _():
            for d in range(2):
                s = probs[3 * d].s_out
                for h in range(GLA_H):
                    ng_ref[d, h] = s[h * GLA_DK:(h + 1) * GLA_DK, h * GLA_DV:(h + 1) * GLA_DV]
                for g in range(HG_GROUPS):
                    s = probs[3 * d + 1 + g].s_out
                    for h in range(HG_GH):
                        nh_ref[d, g * HG_GH + h] = s[h * HG_DK:(h + 1) * HG_DK, h * HG_DV:(h + 1) * HG_DV]


def _scan(proj, gates, mats, lrw_pad, lrb, lbs, s0_gla, s0_hg, *, l, n_seq, emit_final):
    t = proj.shape[0]
    nc = t // n_seq // SCAN_C
    has_init = s0_gla is not None

    def row(d):
        if d == 0:
            return lambda s, c: s * nc + c
        return lambda s, c: s * nc + (nc - 1 - c)

    def col(r, g):
        return lambda s, c: (r(s, c), g)

    in_specs, args = [], []
    for d in range(2):
        r = row(d)
        in_specs += [pl.BlockSpec((SCAN_C, 512), col(r, G_GQK)),
                     pl.BlockSpec((SCAN_C, 512), col(r, G_GV)),
                     pl.BlockSpec((SCAN_C, LANE), col(r, G_GZ)),
                     pl.BlockSpec((SCAN_C, 512), col(r, G_HQ)),
                     pl.BlockSpec((SCAN_C, 512), col(r, G_HFF if d == 0 else G_HFB)),
                     pl.BlockSpec((SCAN_C, 512), col(r, G_HI))]
        args += [proj, proj, gates, proj, gates, proj]
    in_specs += [pl.BlockSpec((2, R_ROWS, SCAN_C), lambda s, c: (0, 0, 0)),
                 pl.BlockSpec((None, 2, LANE, GLA_QK), lambda s, c: (l, 0, 0, 0)),
                 pl.BlockSpec((None, 2, 1, GLA_QK), lambda s, c: (l, 0, 0, 0)),
                 pl.BlockSpec((2, None, 1, HG_K), lambda s, c: (0, l, 0, 0))]
    args += [mats, lrw_pad, lrb, lbs]
    if has_init:
        in_specs += [pl.BlockSpec((None, None, 2, GLA_H, GLA_DK, GLA_DV), lambda s, c: (s, l, 0, 0, 0, 0)),
                     pl.BlockSpec((None, None, 2, HG_H, HG_DK, HG_DV), lambda s, c: (s, l, 0, 0, 0, 0))]
        args += [s0_gla, s0_hg]
    out_specs = [pl.BlockSpec((SCAN_C, 512), col(row(d), 0)) for d in (0, 1, 0, 1)]
    out_shape = [jax.ShapeDtypeStruct((t, 512), BF16)] * 4
    if emit_final:
        out_specs += [pl.BlockSpec((None, 2, GLA_H, GLA_DK, GLA_DV), lambda s, c: (s, 0, 0, 0, 0)),
                      pl.BlockSpec((None, 2, HG_H, HG_DK, HG_DV), lambda s, c: (s, 0, 0, 0, 0))]
        out_shape += [jax.ShapeDtypeStruct((n_seq, 2, GLA_H, GLA_DK, GLA_DV), F32),
                      jax.ShapeDtypeStruct((n_seq, 2, HG_H, HG_DK, HG_DV), F32)]
    return pl.pallas_call(
        functools.partial(_scan_kernel, has_init=has_init, emit_final=emit_final),
        grid=(n_seq, nc),
        in_specs=in_specs,
        out_specs=out_specs,
        out_shape=out_shape,
        scratch_shapes=[pltpu.VMEM((2, GROUP_K, GLA_V), F32),
                        pltpu.VMEM((2, HG_GROUPS, GROUP_K, HG_GV), F32)],
        compiler_params=_cparams(("parallel", "arbitrary")),
        name="scan",
    )(*args)


def _head_norm_gate(o, g, gate, n_heads, dv):
    parts = []
    for h in range(n_heads):
        oh = o[:, h * dv:(h + 1) * dv]
        parts.append(oh * lax.rsqrt(jnp.mean(oh * oh, axis=-1, keepdims=True) + EPS))
    of = jnp.concatenate(parts, axis=1)
    return of * g * (gate * _sigmoid(gate))


def _merge_kernel(x_ref, ogf_ref, ogb_ref, ohf_ref, ohb_ref, go_ref, ho_ref, muv_ref, agla_ref, ahg_ref, agm_ref,
                  mod_ref, gng_ref, hng_ref, gmg_ref, ws_ref, bias_ref, wbg_ref, wbh_ref, wbm_ref, wout_ref,
                  out_ref):
    o_gla = _head_norm_gate(ogf_ref[...].astype(F32) + ogb_ref[...].astype(F32), gng_ref[...],
                            go_ref[...].astype(F32), GLA_H, GLA_DV)
    o_hg = _head_norm_gate(ohf_ref[...].astype(F32) + ohb_ref[...].astype(F32), hng_ref[...],
                           ho_ref[...].astype(F32), HG_H, HG_DV)

    u = _gelu_tanh(muv_ref[:, :GM_W].astype(F32))
    v = _gelu_tanh(muv_ref[:, GM_W:].astype(F32))
    mu = jnp.mean(v, axis=-1, keepdims=True)
    vc = v - mu
    var = jnp.mean(vc * vc, axis=-1, keepdims=True)
    vn = ((vc * lax.rsqrt(var + EPS)) * gmg_ref[...]).astype(BF16)
    rows = []
    for n in range(vn.shape[0] // GM_CHUNK):
        cols = [_dot(ws_ref[g], vn[n * GM_CHUNK:(n + 1) * GM_CHUNK, g * GM_CH:(g + 1) * GM_CH]) for g in range(GM_G)]
        rows.append(jnp.concatenate(cols, axis=1) + bias_ref[...])
    mixed = rows[0] if len(rows) == 1 else jnp.concatenate(rows, axis=0)
    o_gm = u * mixed

    merged = (_sigmoid(agla_ref[...].astype(F32)) * _dot(o_gla.astype(BF16), wbg_ref[...])
              + _sigmoid(ahg_ref[...].astype(F32)) * _dot(o_hg.astype(BF16), wbh_ref[...])
              + _sigmoid(agm_ref[...].astype(F32)) * _dot(o_gm.astype(BF16), wbm_ref[...]))
    mix = _dot(merged.astype(BF16), wout_ref[...])
    out_ref[...] = x_ref[...] + mod_ref[2:3, :] * mix


def _merge(x, og_f, og_b, oh_f, oh_b, proj, mods, p, l, mod_row, tm):
    t, d = x.shape
    row_spec = lambda w, col: pl.BlockSpec((tm, w), lambda i: (i, col))
    lay = lambda *shape: pl.BlockSpec((None,) + shape, lambda i: (l,) + (0,) * len(shape))
    return pl.pallas_call(
        _merge_kernel,
        grid=(t // tm,),
        in_specs=[row_spec(d, 0),
                  row_spec(GLA_V, 0), row_spec(GLA_V, 0), row_spec(HG_V, 0), row_spec(HG_V, 0),
                  row_spec(512, G_GO), row_spec(512, G_HO), row_spec(1024, G_MUV),
                  row_spec(1024, G_AGLA), row_spec(1024, G_AHG), row_spec(1024, G_AGM),
                  pl.BlockSpec((None, None, 6, d), lambda i: (l, mod_row(i), 0, 0)),
                  lay(1, GLA_V), lay(1, HG_V), lay(1, GM_W),
                  lay(GM_G, GM_CHUNK, GM_CHUNK), lay(GM_CHUNK, GM_W),
                  lay(GLA_V, d), lay(HG_V, d), lay(GM_W, d), lay(d, d)],
        out_specs=pl.BlockSpec((tm, d), lambda i: (i, 0)),
        out_shape=jax.ShapeDtypeStruct((t, d), F32),
        compiler_params=_cparams(("parallel",)),
        name="merge",
    )(x, og_f, og_b, oh_f, oh_b, proj, proj, proj, proj, proj, proj, mods,
      p["gla_norm_g"], p["hg_norm_g"], p["gm_norm_g"], p["gm_ws"], p["gm_bias"],
      p["w_br_gla"], p["w_br_hg"], p["w_br_gm"], p["w_out"])


def _ffn_kernel(x_ref, mod_ref, g_ref, w1_ref, w2_ref, out_ref, h_ref, acc_ref):
    j = pl.program_id(1)

    @pl.when(j == 0)
    def _():
        x = x_ref[...]
        ms = jnp.mean(x * x, axis=-1, keepdims=True)
        hn = (x * lax.rsqrt(ms + EPS)) * g_ref[...]
        h_ref[...] = (hn * (1.0 + mod_ref[4:5, :]) + mod_ref[3:4, :]).astype(BF16)
        acc_ref[...] = jnp.zeros_like(acc_ref)

    a = jnp.maximum(_dot(h_ref[...], w1_ref[...]), 0.0)
    acc_ref[...] += _dot((a * a).astype(BF16), w2_ref[...])

    @pl.when(j == pl.num_programs(1) - 1)
    def _():
        out_ref[...] = x_ref[...] + mod_ref[5:6, :] * acc_ref[...]


def _ffn(x, mods, g, w1, w2, l, mod_row, tm):
    t, d = x.shape
    tf = 1024
    return pl.pallas_call(
        _ffn_kernel,
        grid=(t // tm, D_FF // tf),
        in_specs=[pl.BlockSpec((tm, d), lambda i, j: (i, 0)),
                  pl.BlockSpec((None, None, 6, d), lambda i, j: (l, mod_row(i), 0, 0)),
                  pl.BlockSpec((None, 1, d), lambda i, j: (l, 0, 0)),
                  pl.BlockSpec((None, d, tf), lambda i, j: (l, 0, j)),
                  pl.BlockSpec((None, tf, d), lambda i, j: (l, j, 0))],
        out_specs=pl.BlockSpec((tm, d), lambda i, j: (i, 0)),
        out_shape=jax.ShapeDtypeStruct((t, d), F32),
        scratch_shapes=[pltpu.VMEM((tm, d), BF16), pltpu.VMEM((tm, d), F32)],
        compiler_params=_cparams(("parallel", "arbitrary")),
        name="ffn",
    )(x, mods, g, w1, w2)


def _final_norm_kernel(x_ref, g_ref, o_ref):
    x = x_ref[...]
    ms = jnp.mean(x * x, axis=-1, keepdims=True)
    o_ref[...] = (x * lax.rsqrt(ms + EPS)) * g_ref[...]


def _final_norm(x, g):
    t, d = x.shape
    tm = min(ROW_TILE, t)
    return pl.pallas_call(
        _final_norm_kernel,
        grid=(t // tm,),
        in_specs=[pl.BlockSpec((tm, d), lambda i: (i, 0)), pl.BlockSpec((1, d), lambda i: (0, 0))],
        out_specs=pl.BlockSpec((tm, d), lambda i: (i, 0)),
        out_shape=jax.ShapeDtypeStruct((t, d), F32),
        compiler_params=_cparams(("parallel",)),
        name="final_norm",
    )(x, g.reshape(1, d))


def _grid_posemb(n_tok):
    rows = n_tok // GRID_W
    r, col = jnp.meshgrid(jnp.arange(rows, dtype=F32), jnp.arange(GRID_W, dtype=F32), indexing="ij")
    r = r.reshape(-1)
    col = col.reshape(-1)
    nf = D_MODEL // 4
    omega = 1.0 / (10000.0 ** (jnp.arange(nf, dtype=F32) / nf))
    er = r[:, None] * omega
    ec = col[:, None] * omega
    return jnp.concatenate([jnp.sin(er), jnp.cos(er), jnp.sin(ec), jnp.cos(ec)], axis=-1)


def _tile_rows(n_rows, seq_rows, shared_mod):
    return min(ROW_TILE, n_rows if shared_mod else seq_rows)


def kernel(x_prompt, x_sample, c, state_gla, state_hgrn, c_ctx, ada_w, ada_b, norm_mix_g, norm_ffn_g, w_in, gla_lr_w,
           gla_lr_b, gla_norm_g, hg_lb_logits, hg_norm_g, gm_norm_g, gm_ws, gm_bs, w_br_gla, w_br_hg, w_br_gm, w_out,
           w_ff1, w_ff2, final_norm_g):
    batch, seq, d = x_prompt.shape
    dec_batch, dec_seq, _ = x_sample.shape
    depth = ada_w.shape[0]

    starts = dict(zip(SPLIT_NAMES, np.cumsum((0,) + SPLIT_SIZES[:-1])))
    sizes = dict(zip(SPLIT_NAMES, SPLIT_SIZES))
    cols = lambda names: [w_in[..., int(starts[n]):int(starts[n]) + sizes[n]] for n in names]
    w_main = jnp.concatenate(cols(MAIN_COLS), axis=-1).astype(BF16)
    w_gate = jnp.concatenate(cols(GATE_COLS) + [jnp.zeros((depth, d, LANE - 2 * GLA_RANK), F32)], axis=-1).astype(BF16)
    lrw_pad = jnp.stack(
        [jnp.pad(gla_lr_w[:, dd], ((0, 0), (dd * GLA_RANK, LANE - (dd + 1) * GLA_RANK), (0, 0))) for dd in range(2)],
        axis=1).astype(BF16)
    lrb = gla_lr_b.reshape(depth, 2, 1, GLA_QK)
    gm_bias = jnp.broadcast_to(jnp.swapaxes(gm_bs, 1, 2)[:, :, :, None],
                               (depth, GM_CHUNK, GM_G, GM_CH)).reshape(depth, GM_CHUNK, GM_W)
    params = {
        "gla_norm_g": gla_norm_g.reshape(depth, 1, GLA_V),
        "hg_norm_g": hg_norm_g.reshape(depth, 1, HG_V),
        "gm_norm_g": gm_norm_g.reshape(depth, 1, GM_W),
        "gm_ws": gm_ws.astype(BF16),
        "gm_bias": gm_bias,
        "w_br_gla": w_br_gla.astype(BF16),
        "w_br_hg": w_br_hg.astype(BF16),
        "w_br_gm": w_br_gm.astype(BF16),
        "w_out": w_out.astype(BF16),
    }
    w1 = w_ff1.astype(BF16)
    w2 = w_ff2.astype(BF16)
    g_mix = norm_mix_g.reshape(depth, 1, d)
    g_ffn = norm_ffn_g.reshape(depth, 1, d)
    mats = _scan_matrices()

    n_mod = 1 + dec_batch
    n_mod_pad = -(-n_mod // 8) * 8
    cvec = jnp.concatenate([c_ctx[None, :], c, jnp.zeros((n_mod_pad - n_mod, d), F32)], axis=0)
    mods = _modulations(cvec, ada_w, ada_b).reshape(depth, n_mod_pad, 6, d)
    lbs = _lower_bounds(hg_lb_logits).reshape(2, depth, 1, HG_K)

    xc = x_prompt.reshape(batch * seq, d)
    xl = _embed(x_sample, _grid_posemb(dec_seq)).reshape(dec_batch * dec_seq, d)

    tm_c = _tile_rows(batch * seq, seq, True)
    tm_l = _tile_rows(dec_batch * dec_seq, dec_seq, False)
    tmm_c = min(MERGE_TILE, seq)
    tmm_l = min(MERGE_TILE, dec_seq)
    ctx_row = lambda i: 0
    lat_row = lambda tm: (lambda i: 1 + (i * tm) // dec_seq)
    new_gla = []
    new_hg = []
    for l in range(depth):
        proj, gz = _inproj(xc, mods, g_mix, w_main, w_gate, l, ctx_row, tm_c)
        og_f, og_b, oh_f, oh_b, ng, nh = _scan(proj, gz, mats, lrw_pad, lrb, lbs, None, None, l=l, n_seq=batch,
                                               emit_final=True)
        new_gla.append(ng)
        new_hg.append(nh)
        xc = _merge(xc, og_f, og_b, oh_f, oh_b, proj, mods, params, l, ctx_row, tmm_c)
        xc = _ffn(xc, mods, g_ffn, w1, w2, l, ctx_row, tm_c)

        proj, gz = _inproj(xl, mods, g_mix, w_main, w_gate, l, lat_row(tm_l), tm_l)
        og_f, og_b, oh_f, oh_b = _scan(proj, gz, mats, lrw_pad, lrb, lbs, state_gla, state_hgrn, l=l,
                                       n_seq=dec_batch, emit_final=False)
        xl = _merge(xl, og_f, og_b, oh_f, oh_b, proj, mods, params, l, lat_row(tmm_l), tmm_l)
        xl = _ffn(xl, mods, g_ffn, w1, w2, l, lat_row(tm_l), tm_l)

    y_prompt = _final_norm(xc, final_norm_g).reshape(batch, seq, d)
    y_sample = _final_norm(xl, final_norm_g).reshape(dec_batch, dec_seq, d)
    return (y_prompt, y_sample, jnp.stack(new_gla, axis=1), jnp.stack(new_hg, axis=1))
```

```python
import functools

import jax
import jax.numpy as jnp
import numpy as np
from jax import lax
from jax.experimental import pallas as pl
from jax.experimental.pallas import tpu as pltpu

F32 = jnp.float32
BF16 = jnp.bfloat16

D_MODEL = 1024
GRID_W = 64
GLA_H, GLA_DK, GLA_DV, GLA_RANK = 4, 64, 128, 16
GLA_GATE_NORM = 16.0
HG_H, HG_DK, HG_DV = 4, 128, 128
GM_G, GM_CH, GM_CHUNK = 4, 128, 128
D_FF = 4 * D_MODEL
EPS = 1e-6
GLA_QK = GLA_H * GLA_DK
GLA_V = GLA_H * GLA_DV
HG_K = HG_H * HG_DK
HG_V = HG_H * HG_DV
GM_W = GM_G * GM_CH
SPLIT_SIZES = (GLA_QK, GLA_QK, GLA_V, GLA_V, GLA_RANK, GLA_RANK, HG_K, HG_K, HG_K, HG_V, HG_V, GM_W, GM_W,
               D_MODEL, D_MODEL, D_MODEL)
SPLIT_NAMES = ("gq", "gk", "gv", "go", "gzf", "gzb", "hq", "hff", "hfb", "hi", "ho", "mu", "mv",
               "a_gla", "a_hg", "a_gm")
MAIN_COLS = ("gq", "gk", "gv", "go", "hq", "hi", "ho", "mu", "mv", "a_gla", "a_hg", "a_gm")
GATE_COLS = ("hff", "hfb", "gzf", "gzb")
D_MAIN = 7168
D_GATE = 2 * HG_K + 128

G_GQK, G_GV, G_GO, G_HQ, G_HI, G_HO = range(6)
G_MUV = 3
G_AGLA, G_AHG, G_AGM = 4, 5, 6
G_HFF, G_HFB = 0, 1
G_GZ = 2 * HG_K // 128

LANE = 128
SCAN_C = 128
GROUP_K = 256
SUB = 16
N_SUB = SCAN_C // SUB
LEVELS = (4, 2, 1)
ROW_TILE = 1024
INPROJ_TN = 1792
MERGE_TILE = 512
VMEM_LIMIT = 56 * 1024 * 1024

HG_GROUPS = HG_K // GROUP_K
HG_GH = HG_H // HG_GROUPS
HG_GV = HG_GH * HG_DV


def _cparams(sem):
    return pltpu.CompilerParams(dimension_semantics=sem, vmem_limit_bytes=VMEM_LIMIT)


def _dot(a, b):
    return jnp.dot(a, b, preferred_element_type=F32)


def _dot_nt(a, b):
    return lax.dot_general(a, b, (((1,), (1,)), ((), ())), preferred_element_type=F32)


def _dot_tn(a, b):
    return lax.dot_general(a, b, (((0,), (0,)), ((), ())), preferred_element_type=F32)


def _sigmoid(x):
    return 0.5 + 0.5 * jnp.tanh(0.5 * x)


def _log_sigmoid(x):
    return jnp.minimum(x, 0.0) - jnp.log(1.0 + jnp.exp(-jnp.abs(x)))


def _gelu_tanh(x):
    c = np.sqrt(2.0 / np.pi).astype(np.float32)
    return x * (0.5 * (1.0 + jnp.tanh(c * (x + 0.044715 * (x * x * x)))))


def _mod_kernel(c_ref, w_ref, b_ref, o_ref):
    c = c_ref[...]
    s = (c * _sigmoid(c)).astype(BF16)
    o_ref[...] = _dot(s, w_ref[...].astype(BF16)) + b_ref[...]


def _modulations(cvec, ada_w, ada_b):
    depth, d, d6 = ada_w.shape
    r = cvec.shape[0]
    tn = 1024
    return pl.pallas_call(
        _mod_kernel,
        grid=(depth, d6 // tn),
        in_specs=[pl.BlockSpec((r, d), lambda l, j: (0, 0)),
                  pl.BlockSpec((None, d, tn), lambda l, j: (l, 0, j)),
                  pl.BlockSpec((None, 1, tn), lambda l, j: (l, 0, j))],
        out_specs=pl.BlockSpec((None, r, tn), lambda l, j: (l, 0, j)),
        out_shape=jax.ShapeDtypeStruct((depth, r, d6), F32),
        compiler_params=_cparams(("parallel", "parallel")),
        name="adaln_mod",
    )(cvec, ada_w, ada_b.reshape(depth, 1, d6))


def _lb_kernel(x_ref, o_ref):
    x = x_ref[...]
    m = jnp.max(x, axis=1, keepdims=True)
    e = jnp.exp(x - m)
    p = e / jnp.sum(e, axis=1, keepdims=True)
    acc = jnp.zeros_like(p[:, 0:1, :])
    o_ref[:, 0:1, :] = acc
    for l in range(1, x.shape[1]):
        acc = acc + p[:, l:l + 1, :]
        o_ref[:, l:l + 1, :] = acc


def _lower_bounds(logits):
    return pl.pallas_call(
        _lb_kernel,
        out_shape=jax.ShapeDtypeStruct(logits.shape, F32),
        name="hgrn_lower_bounds",
    )(logits)


def _inproj_kernel(*refs, embed):
    if embed:
        x_ref, pos_ref, mod_ref, g_ref, w_ref, wgate_ref, out_ref, gate_ref, x0_ref, h_ref = refs
    else:
        x_ref, mod_ref, g_ref, w_ref, wgate_ref, out_ref, gate_ref, h_ref = refs

    @pl.when(pl.program_id(1) == 0)
    def _():
        x = x_ref[...]
        if embed:
            x = x + pos_ref[...]
            x0_ref[...] = x
        ms = jnp.mean(x * x, axis=-1, keepdims=True)
        hn = (x * lax.rsqrt(ms + EPS)) * g_ref[...]
        h = (hn * (1.0 + mod_ref[1:2, :]) + mod_ref[0:1, :]).astype(BF16)
        h_ref[...] = h
        gate_ref[...] = _dot(h, wgate_ref[...])

    out_ref[...] = _dot(h_ref[...], w_ref[...]).astype(out_ref.dtype)


def _inproj(x, mods, g, w_main, w_gate, l, mod_row, tm, pos=None):
    t, d = x.shape
    tn = INPROJ_TN
    embed = pos is not None
    pos_specs, pos_args, x0_specs, x0_shape = [], [], [], []
    if embed:
        tiles_per_seq = pos.shape[0] // tm
        pos_specs = [pl.BlockSpec((tm, d), lambda i, j: (i % tiles_per_seq, 0))]
        pos_args = [pos]
        x0_specs = [pl.BlockSpec((tm, d), lambda i, j: (i, 0))]
        x0_shape = [jax.ShapeDtypeStruct((t, d), F32)]
    return pl.pallas_call(
        functools.partial(_inproj_kernel, embed=embed),
        grid=(t // tm, D_MAIN // tn),
        in_specs=[pl.BlockSpec((tm, d), lambda i, j: (i, 0))] + pos_specs + [
                  pl.BlockSpec((None, None, 6, d), lambda i, j: (l, mod_row(i), 0, 0)),
                  pl.BlockSpec((None, 1, d), lambda i, j: (l, 0, 0)),
                  pl.BlockSpec((None, d, tn), lambda i, j: (l, 0, j)),
                  pl.BlockSpec((None, d, D_GATE), lambda i, j: (l, 0, 0))],
        out_specs=[pl.BlockSpec((tm, tn), lambda i, j: (i, j)),
                   pl.BlockSpec((tm, D_GATE), lambda i, j: (i, 0))] + x0_specs,
        out_shape=[jax.ShapeDtypeStruct((t, D_MAIN), BF16),
                   jax.ShapeDtypeStruct((t, D_GATE), F32)] + x0_shape,
        scratch_shapes=[pltpu.VMEM((tm, d), BF16)],
        compiler_params=_cparams(("parallel", "arbitrary")),
        name="inproj",
    )(x, *pos_args, mods, g, w_main, w_gate)


R_LOCAL = 0
R_BLOCK = SCAN_C
R_PRE = R_BLOCK + N_SUB
R_POST = R_PRE + N_SUB
R_LEVEL = R_POST + N_SUB
R_TOTAL = R_LEVEL + len(LEVELS) * N_SUB
R_ROWS = -(-(R_TOTAL + N_SUB) // 16) * 16


def _scan_matrices():
    pos = np.arange(SCAN_C)
    blk = pos // SUB
    out = np.zeros((2, R_ROWS, SCAN_C), np.float32)
    for d, fwd in enumerate((True, False)):
        order = np.arange(N_SUB) if fwd else N_SUB - 1 - np.arange(N_SUB)
        earlier_eq = (pos[None, :] <= pos[:, None]) if fwd else (pos[None, :] >= pos[:, None])
        out[d, R_LOCAL:R_LOCAL + SCAN_C] = (blk[:, None] == blk[None, :]) & earlier_eq
        um = order[:, None]
        un = order[None, :]

        def put(row0, cond):
            out[d, row0:row0 + N_SUB] = cond[:, blk]

        put(R_BLOCK, um == un)
        put(R_PRE, un < um)
        put(R_POST, un > um)
        for i, bb in enumerate(LEVELS):
            sb = 2 * bb
            w, wn = um % sb, un % sb
            q_side = w >= bb
            inside = np.where(q_side, (wn >= bb) & (wn < w), (wn > w) & (wn < bb))
            put(R_LEVEL + i * N_SUB, ((um // sb) == (un // sb)) & inside)
        put(R_TOTAL, np.ones((N_SUB, N_SUB), bool))
    return jnp.asarray(out, BF16)


def _per_block(x8):
    return jnp.concatenate([jnp.broadcast_to(x8[m:m + 1, :], (SUB, x8.shape[1])) for m in range(N_SUB)], axis=0)


def _div_p2(x, n):
    return lax.shift_right_logical(x, int(n).bit_length() - 1)


def _mod_p2(x, n):
    return lax.bitwise_and(x, n - 1)


class _Problem:
    def __init__(self, fwd, hg, dv, q, k, la, v, s_ref, s_idx, o_ref, o_cols):
        self.fwd, self.hg, self.dv = fwd, hg, dv
        self.q, self.k, self.la, self.v = q, k, la, v
        self.s_ref, self.s_idx, self.o_ref, self.o_cols = s_ref, s_idx, o_ref, o_cols


def _stack_heads(b, hg):
    c = b.shape[0]
    dk = GROUP_K // hg
    per_lane = LANE // dk
    zb = jnp.zeros((c, LANE), BF16)
    blocks = []
    for h in range(hg):
        lv = (h * dk) // LANE
        piece = b[:, lv * LANE:(lv + 1) * LANE]
        if per_lane > 1:
            sub_head = _div_p2(lax.broadcasted_iota(jnp.int32, (c, LANE), 1), dk)
            piece = jnp.where(sub_head == (h % per_lane), piece, 0.0)
        piece = piece.astype(BF16)
        blocks.append(jnp.concatenate([piece if j == lv else zb for j in range(GROUP_K // LANE)], axis=1))
    return jnp.concatenate(blocks, axis=0)


def _stage_factors(p, r):
    c = SCAN_C
    local = r[R_LOCAL:R_LOCAL + c]
    block_total = _per_block(r[R_BLOCK:R_BLOCK + N_SUB])
    small = jnp.exp(r[R_PRE:R_TOTAL])
    pre = _per_block(small[0:N_SUB])
    post = _per_block(small[N_SUB:2 * N_SUB])
    ql = p.q * jnp.exp(local)
    kr = p.k * jnp.exp(block_total - local)
    ki = p.k * jnp.exp(-local)
    p.q_t = (ql * pre).astype(BF16)
    p.k_t = (kr * post).astype(BF16)
    ql_b = ql.astype(BF16)
    masked = GROUP_K // p.hg < LANE
    kr_b = kr if masked else kr.astype(BF16)
    p.a_ops = [ql_b]
    p.b_ops = [_stack_heads(ki, p.hg)]
    for i in range(len(LEVELS)):
        f = _per_block(small[(2 + i) * N_SUB:(3 + i) * N_SUB])
        f_b = f.astype(BF16)
        p.a_ops.append(ql_b * f_b)
        p.b_ops.append(_stack_heads(kr_b * (f if masked else f_b), p.hg))
    tot_rows = jnp.concatenate([r[R_TOTAL:R_TOTAL + N_SUB]] * (GROUP_K // N_SUB), axis=0)
    eye = (lax.broadcasted_iota(jnp.int32, (GROUP_K, GROUP_K), 0)
           == lax.broadcasted_iota(jnp.int32, (GROUP_K, GROUP_K), 1))
    p.decay_col = jnp.exp(jnp.sum(jnp.where(eye, tot_rows, 0.0), axis=1, keepdims=True))


def _stage_products(p):
    p.s_in = p.s_ref[p.s_idx]
    p.o_inter = _dot(p.q_t, p.s_in.astype(BF16))
    p.kv = _dot_tn(p.k_t, p.v)
    p.prods = [_dot_nt(a, b) for a, b in zip(p.a_ops, p.b_ops)]


def _stage_finish(p):
    c, hg, dv = SCAN_C, p.hg, p.dv
    dk = GROUP_K // hg
    si = lax.broadcasted_iota(jnp.int32, (c, hg * c), 0)
    sj = _mod_p2(lax.broadcasted_iota(jnp.int32, (c, hg * c), 1), c)
    m = (_div_p2(si, SUB) == _div_p2(sj, SUB)) & ((sj <= si) if p.fwd else (sj >= si))
    scores = jnp.where(m, p.prods[0], 0.0)
    for i, bb in enumerate(LEVELS):
        blk = bb * SUB
        sb = 2 * blk
        i_second = _mod_p2(si, sb) >= blk
        j_second = _mod_p2(sj, sb) >= blk
        same = _div_p2(si, sb) == _div_p2(sj, sb)
        if p.fwd:
            m = same & i_second & jnp.logical_not(j_second)
        else:
            m = same & jnp.logical_not(i_second) & j_second
        scores = jnp.where(m, p.prods[1 + i], scores)

    zv = jnp.zeros((c, dv), BF16)
    v_bd = jnp.concatenate(
        [jnp.concatenate([p.v[:, h * dv:(h + 1) * dv] if j == h else zv for j in range(hg)], axis=1)
         for h in range(hg)], axis=0)
    o = p.o_inter + _dot(scores.astype(BF16), v_bd)
    p.o_ref[:, p.o_cols] = o.astype(p.o_ref.dtype)

    kv_bd = _blockdiag([p.kv[h * dk:(h + 1) * dk, h * dv:(h + 1) * dv] for h in range(hg)])
    p.s_out = p.s_in * p.decay_col + kv_bd
    p.s_ref[p.s_idx] = p.s_out


def _blockdiag(blocks):
    hg = len(blocks)
    z = jnp.zeros_like(blocks[0])
    return jnp.concatenate(
        [jnp.concatenate([blocks[h] if j == h else z for j in range(hg)], axis=1) for h in range(hg)], axis=0)


def _scan_kernel(*refs, has_init, emit_final):
    tiles = (refs[0:6], refs[6:12])
    mat_ref, lrw_ref, lrb_ref, lb_ref = refs[12:16]
    pos = 16
    if has_init:
        sg0_ref, sh0_ref = refs[pos:pos + 2]
        pos += 2
    og_refs = refs[pos:pos + 2]
    oh_refs = refs[pos + 2:pos + 4]
    pos += 4
    if emit_final:
        ng_ref, nh_ref = refs[pos:pos + 2]
        pos += 2
    sg_ref, sh_ref = refs[pos:pos + 2]

    ci = pl.program_id(1)

    @pl.when(ci == 0)
    def _():
        if has_init:
            for d in range(2):
                sg_ref[d] = _blockdiag([sg0_ref[d, h] for h in range(GLA_H)])
                for g in range(HG_GROUPS):
                    sh_ref[d, g] = _blockdiag([sh0_ref[d, g * HG_GH + h] for h in range(HG_GH)])
        else:
            sg_ref[...] = jnp.zeros_like(sg_ref)
            sh_ref[...] = jnp.zeros_like(sh_ref)

    probs = []
    for d in range(2):
        fwd = d == 0
        gqk_ref, gv_ref, gz_ref, hq_ref, hf_ref, hi_ref = tiles[d]
        z = _dot(gz_ref[...].astype(BF16), lrw_ref[d]) + lrb_ref[d]
        group = [_Problem(fwd, GLA_H, GLA_DV,
                          gqk_ref[:, :GLA_QK].astype(F32) * (GLA_DK ** -0.5), gqk_ref[:, GLA_QK:].astype(F32),
                          _log_sigmoid(z) * (1.0 / GLA_GATE_NORM), gv_ref[...],
                          sg_ref, (d,), og_refs[d], slice(0, GLA_V))]
        for g in range(HG_GROUPS):
            ks = slice(g * GROUP_K, (g + 1) * GROUP_K)
            vs = slice(g * HG_GV, (g + 1) * HG_GV)
            lb = lb_ref[d, :, ks]
            sig = 1.0 / (1.0 + jnp.exp(-hf_ref[:, ks]))
            group.append(_Problem(fwd, HG_GH, HG_DV,
                                  hq_ref[:, ks].astype(F32) * (HG_DK ** -0.5), (1.0 - lb) * (1.0 - sig),
                                  jnp.log(lb + (1.0 - lb) * sig), hi_ref[:, vs],
                                  sh_ref, (d, g), oh_refs[d], vs))
        la = jnp.concatenate([p.la for p in group], axis=1)
        la_hi = la.astype(BF16)
        la_lo = (la - la_hi.astype(F32)).astype(BF16)
        r = _dot(mat_ref[d], la_hi) + _dot(mat_ref[d], la_lo)
        for i, p in enumerate(group):
            p.r = r[:, i * GROUP_K:(i + 1) * GROUP_K]
        probs += group

    for p in probs:
        _stage_factors(p, p.r)
    for p in probs:
        _stage_products(p)
    for p in probs:
        _stage_finish(p)

    if emit_final:
        @pl.when(ci == pl.num_programs(1) - 1)
        def _():
            for d in range(2):
                s = probs[3 * d].s_out
                for h in range(GLA_H):
                    ng_ref[d, h] = s[h * GLA_DK:(h + 1) * GLA_DK, h * GLA_DV:(h + 1) * GLA_DV]
                for g in range(HG_GROUPS):
                    s = probs[3 * d + 1 + g].s_out
                    for h in range(HG_GH):
                        nh_ref[d, g * HG_GH + h] = s[h * HG_DK:(h + 1) * HG_DK, h * HG_DV:(h + 1) * HG_DV]


def _scan(proj, gates, mats, lrw_pad, lrb, lbs, s0_gla, s0_hg, *, l, n_seq, emit_final):
    t = proj.shape[0]
    nc = t // n_seq // SCAN_C
    has_init = s0_gla is not None

    def row(d):
        if d == 0:
            return lambda s, c: s * nc + c
        return lambda s, c: s * nc + (nc - 1 - c)

    def col(r, g):
        return lambda s, c: (r(s, c), g)

    in_specs, args = [], []
    for d in range(2):
        r = row(d)
        in_specs += [pl.BlockSpec((SCAN_C, 512), col(r, G_GQK)),
                     pl.BlockSpec((SCAN_C, 512), col(r, G_GV)),
                     pl.BlockSpec((SCAN_C, LANE), col(r, G_GZ)),
                     pl.BlockSpec((SCAN_C, 512), col(r, G_HQ)),
                     pl.BlockSpec((SCAN_C, 512), col(r, G_HFF if d == 0 else G_HFB)),
                     pl.BlockSpec((SCAN_C, 512), col(r, G_HI))]
        args += [proj, proj, gates, proj, gates, proj]
    in_specs += [pl.BlockSpec((2, R_ROWS, SCAN_C), lambda s, c: (0, 0, 0)),
                 pl.BlockSpec((None, 2, LANE, GLA_QK), lambda s, c: (l, 0, 0, 0)),
                 pl.BlockSpec((None, 2, 1, GLA_QK), lambda s, c: (l, 0, 0, 0)),
                 pl.BlockSpec((2, None, 1, HG_K), lambda s, c: (0, l, 0, 0))]
    args += [mats, lrw_pad, lrb, lbs]
    if has_init:
        in_specs += [pl.BlockSpec((None, None, 2, GLA_H, GLA_DK, GLA_DV), lambda s, c: (s, l, 0, 0, 0, 0)),
                     pl.BlockSpec((None, None, 2, HG_H, HG_DK, HG_DV), lambda s, c: (s, l, 0, 0, 0, 0))]
        args += [s0_gla, s0_hg]
    out_specs = [pl.BlockSpec((SCAN_C, 512), col(row(d), 0)) for d in (0, 1, 0, 1)]
    out_shape = [jax.ShapeDtypeStruct((t, 512), BF16)] * 4
    if emit_final:
        out_specs += [pl.BlockSpec((None, 2, GLA_H, GLA_DK, GLA_DV), lambda s, c: (s, 0, 0, 0, 0)),
                      pl.BlockSpec((None, 2, HG_H, HG_DK, HG_DV), lambda s, c: (s, 0, 0, 0, 0))]
        out_shape += [jax.ShapeDtypeStruct((n_seq, 2, GLA_H, GLA_DK, GLA_DV), F32),
                      jax.ShapeDtypeStruct((n_seq, 2, HG_H, HG_DK, HG_DV), F32)]
    return pl.pallas_call(
        functools.partial(_scan_kernel, has_init=has_init, emit_final=emit_final),
        grid=(n_seq, nc),
        in_specs=in_specs,
        out_specs=out_specs,
        out_shape=out_shape,
        scratch_shapes=[pltpu.VMEM((2, GROUP_K, GLA_V), F32),
                        pltpu.VMEM((2, HG_GROUPS, GROUP_K, HG_GV), F32)],
        compiler_params=_cparams(("parallel", "arbitrary")),
        name="scan",
    )(*args)


def _head_norm_gate(o_fwd, o_bwd, g, gate, n_heads, dv):
    o = (o_fwd + o_bwd).astype(F32)
    half = 0.5 * gate
    silu = (half + half * jnp.tanh(half)).astype(F32)
    parts = []
    for h in range(n_heads):
        oh = o[:, h * dv:(h + 1) * dv]
        parts.append(oh * lax.rsqrt(jnp.mean(oh * oh, axis=-1, keepdims=True) + EPS))
    return jnp.concatenate(parts, axis=1) * g * silu


def _merge_kernel(x_ref, ogf_ref, ogb_ref, ohf_ref, ohb_ref, go_ref, ho_ref, muv_ref, agla_ref, ahg_ref, agm_ref,
                  mod_ref, gng_ref, hng_ref, gmg_ref, ws_ref, bias_ref, wbg_ref, wbh_ref, wbm_ref, wout_ref,
                  out_ref):
    o_gla = _head_norm_gate(ogf_ref[...], ogb_ref[...], gng_ref[...], go_ref[...], GLA_H, GLA_DV)
    o_hg = _head_norm_gate(ohf_ref[...], ohb_ref[...], hng_ref[...], ho_ref[...], HG_H, HG_DV)

    u = _gelu_tanh(muv_ref[:, :GM_W].astype(F32))
    v = _gelu_tanh(muv_ref[:, GM_W:].astype(F32))
    mu = jnp.mean(v, axis=-1, keepdims=True)
    vc = v - mu
    var = jnp.mean(vc * vc, axis=-1, keepdims=True)
    vn = ((vc * lax.rsqrt(var + EPS)) * gmg_ref[...]).astype(BF16)
    rows = []
    for n in range(vn.shape[0] // GM_CHUNK):
        cols = [_dot(ws_ref[g], vn[n * GM_CHUNK:(n + 1) * GM_CHUNK, g * GM_CH:(g + 1) * GM_CH]) for g in range(GM_G)]
        rows.append(jnp.concatenate(cols, axis=1) + bias_ref[...])
    mixed = rows[0] if len(rows) == 1 else jnp.concatenate(rows, axis=0)
    o_gm = u * mixed

    merged = (_sigmoid(agla_ref[...]) * _dot(o_gla.astype(BF16), wbg_ref[...]).astype(BF16)
              + _sigmoid(ahg_ref[...]) * _dot(o_hg.astype(BF16), wbh_ref[...]).astype(BF16)
              + _sigmoid(agm_ref[...]) * _dot(o_gm.astype(BF16), wbm_ref[...]).astype(BF16))
    mix = _dot(merged, wout_ref[...])
    out_ref[...] = x_ref[...] + mod_ref[2:3, :] * mix


def _merge(x, og_f, og_b, oh_f, oh_b, proj, mods, p, l, mod_row, tm):
    t, d = x.shape
    row_spec = lambda w, col: pl.BlockSpec((tm, w), lambda i: (i, col))
    lay = lambda *shape: pl.BlockSpec((None,) + shape, lambda i: (l,) + (0,) * len(shape))
    return pl.pallas_call(
        _merge_kernel,
        grid=(t // tm,),
        in_specs=[row_spec(d, 0),
                  row_spec(GLA_V, 0), row_spec(GLA_V, 0), row_spec(HG_V, 0), row_spec(HG_V, 0),
                  row_spec(512, G_GO), row_spec(512, G_HO), row_spec(1024, G_MUV),
                  row_spec(1024, G_AGLA), row_spec(1024, G_AHG), row_spec(1024, G_AGM),
                  pl.BlockSpec((None, None, 6, d), lambda i: (l, mod_row(i), 0, 0)),
                  lay(1, GLA_V), lay(1, HG_V), lay(1, GM_W),
                  lay(GM_G, GM_CHUNK, GM_CHUNK), lay(GM_CHUNK, GM_W),
                  lay(GLA_V, d), lay(HG_V, d), lay(GM_W, d), lay(d, d)],
        out_specs=pl.BlockSpec((tm, d), lambda i: (i, 0)),
        out_shape=jax.ShapeDtypeStruct((t, d), F32),
        compiler_params=_cparams(("parallel",)),
        name="merge",
    )(x, og_f, og_b, oh_f, oh_b, proj, proj, proj, proj, proj, proj, mods,
      p["gla_norm_g"], p["hg_norm_g"], p["gm_norm_g"], p["gm_ws"], p["gm_bias"],
      p["w_br_gla"], p["w_br_hg"], p["w_br_gm"], p["w_out"])


def _ffn_kernel(x_ref, mod_ref, g_ref, w1_ref, w2_ref, *rest, final):
    if final:
        gfin_ref, out_ref, h_ref, acc_ref = rest
    else:
        out_ref, h_ref, acc_ref = rest
    j = pl.program_id(1)

    @pl.when(j == 0)
    def _():
        x = x_ref[...]
        ms = jnp.mean(x * x, axis=-1, keepdims=True)
        hn = (x * lax.rsqrt(ms + EPS)) * g_ref[...]
        h_ref[...] = (hn * (1.0 + mod_ref[4:5, :]) + mod_ref[3:4, :]).astype(BF16)
        acc_ref[...] = jnp.zeros_like(acc_ref)

    a = jnp.maximum(_dot(h_ref[...], w1_ref[...]), 0.0)
    acc_ref[...] += _dot((a * a).astype(BF16), w2_ref[...])

    @pl.when(j == pl.num_programs(1) - 1)
    def _():
        y = x_ref[...] + mod_ref[5:6, :] * acc_ref[...]
        if final:
            y = (y * lax.rsqrt(jnp.mean(y * y, axis=-1, keepdims=True) + EPS)) * gfin_ref[...]
        out_ref[...] = y


def _ffn(x, mods, g, w1, w2, l, mod_row, tm, final_g=None):
    t, d = x.shape
    tf = 1024
    final = final_g is not None
    extra_specs = [pl.BlockSpec((1, d), lambda i, j: (0, 0))] if final else []
    extra_args = [final_g.reshape(1, d)] if final else []
    return pl.pallas_call(
        functools.partial(_ffn_kernel, final=final),
        grid=(t // tm, D_FF // tf),
        in_specs=[pl.BlockSpec((tm, d), lambda i, j: (i, 0)),
                  pl.BlockSpec((None, None, 6, d), lambda i, j: (l, mod_row(i), 0, 0)),
                  pl.BlockSpec((None, 1, d), lambda i, j: (l, 0, 0)),
                  pl.BlockSpec((None, d, tf), lambda i, j: (l, 0, j)),
                  pl.BlockSpec((None, tf, d), lambda i, j: (l, j, 0))] + extra_specs,
        out_specs=pl.BlockSpec((tm, d), lambda i, j: (i, 0)),
        out_shape=jax.ShapeDtypeStruct((t, d), F32),
        scratch_shapes=[pltpu.VMEM((tm, d), BF16), pltpu.VMEM((tm, d), F32)],
        compiler_params=_cparams(("parallel", "arbitrary")),
        name="ffn",
    )(x, mods, g, w1, w2, *extra_args)


def _grid_posemb(n_tok):
    rows = n_tok // GRID_W
    r, col = jnp.meshgrid(jnp.arange(rows, dtype=F32), jnp.arange(GRID_W, dtype=F32), indexing="ij")
    r = r.reshape(-1)
    col = col.reshape(-1)
    nf = D_MODEL // 4
    omega = 1.0 / (10000.0 ** (jnp.arange(nf, dtype=F32) / nf))
    er = r[:, None] * omega
    ec = col[:, None] * omega
    return jnp.concatenate([jnp.sin(er), jnp.cos(er), jnp.sin(ec), jnp.cos(ec)], axis=-1)


def _col_ranges(names):
    starts = dict(zip(SPLIT_NAMES, np.cumsum((0,) + SPLIT_SIZES[:-1])))
    sizes = dict(zip(SPLIT_NAMES, SPLIT_SIZES))
    ranges = []
    for n in names:
        a, b = int(starts[n]), int(starts[n]) + sizes[n]
        if ranges and ranges[-1][1] == a:
            ranges[-1] = (ranges[-1][0], b)
        else:
            ranges.append((a, b))
    return ranges


def _tile_rows(n_rows, seq_rows, shared_mod):
    return min(ROW_TILE, n_rows if shared_mod else seq_rows)


def kernel(x_prompt, x_sample, c, state_gla, state_hgrn, c_ctx, ada_w, ada_b, norm_mix_g, norm_ffn_g, w_in, gla_lr_w,
           gla_lr_b, gla_norm_g, hg_lb_logits, hg_norm_g, gm_norm_g, gm_ws, gm_bs, w_br_gla, w_br_hg, w_br_gm, w_out,
           w_ff1, w_ff2, final_norm_g):
    batch, seq, d = x_prompt.shape
    dec_batch, dec_seq, _ = x_sample.shape
    depth = ada_w.shape[0]

    w_main = jnp.concatenate([w_in[..., a:b] for a, b in _col_ranges(MAIN_COLS)], axis=-1).astype(BF16)
    w_gate = jnp.concatenate([w_in[..., a:b] for a, b in _col_ranges(GATE_COLS)]
                             + [jnp.zeros((depth, d, LANE - 2 * GLA_RANK), F32)], axis=-1).astype(BF16)
    lrw_pad = jnp.stack(
        [jnp.pad(gla_lr_w[:, dd], ((0, 0), (dd * GLA_RANK, LANE - (dd + 1) * GLA_RANK), (0, 0))) for dd in range(2)],
        axis=1).astype(BF16)
    lrb = gla_lr_b.reshape(depth, 2, 1, GLA_QK)
    gm_bias = jnp.broadcast_to(jnp.swapaxes(gm_bs, 1, 2)[:, :, :, None],
                               (depth, GM_CHUNK, GM_G, GM_CH)).reshape(depth, GM_CHUNK, GM_W)
    params = {
        "gla_norm_g": gla_norm_g.reshape(depth, 1, GLA_V),
        "hg_norm_g": hg_norm_g.reshape(depth, 1, HG_V),
        "gm_norm_g": gm_norm_g.reshape(depth, 1, GM_W),
        "gm_ws": gm_ws.astype(BF16),
        "gm_bias": gm_bias,
        "w_br_gla": w_br_gla.astype(BF16),
        "w_br_hg": w_br_hg.astype(BF16),
        "w_br_gm": w_br_gm.astype(BF16),
        "w_out": w_out.astype(BF16),
    }
    w1 = w_ff1.astype(BF16)
    w2 = w_ff2.astype(BF16)
    g_mix = norm_mix_g.reshape(depth, 1, d)
    g_ffn = norm_ffn_g.reshape(depth, 1, d)
    mats = _scan_matrices()

    n_mod = 1 + dec_batch
    n_mod_pad = -(-n_mod // 8) * 8
    cvec = jnp.concatenate([c_ctx[None, :], c, jnp.zeros((n_mod_pad - n_mod, d), F32)], axis=0)
    mods = _modulations(cvec, ada_w, ada_b).reshape(depth, n_mod_pad, 6, d)
    lbs = _lower_bounds(hg_lb_logits).reshape(2, depth, 1, HG_K)

    xc = x_prompt.reshape(batch * seq, d)
    xl = x_sample.reshape(dec_batch * dec_seq, d)
    pos = _grid_posemb(dec_seq)

    tm_c = _tile_rows(batch * seq, seq, True)
    tm_l = _tile_rows(dec_batch * dec_seq, dec_seq, False)
    tmm_c = min(MERGE_TILE, batch * seq)
    tmm_l = min(MERGE_TILE, dec_seq)
    ctx_row = lambda i: 0
    lat_row = lambda tm: (lambda i: 1 + (i * tm) // dec_seq)
    new_gla = []
    new_hg = []
    for l in range(depth):
        proj, gz = _inproj(xc, mods, g_mix, w_main, w_gate, l, ctx_row, tm_c)
        og_f, og_b, oh_f, oh_b, ng, nh = _scan(proj, gz, mats, lrw_pad, lrb, lbs, None, None, l=l, n_seq=batch,
                                               emit_final=True)
        new_gla.append(ng)
        new_hg.append(nh)
        xc = _merge(xc, og_f, og_b, oh_f, oh_b, proj, mods, params, l, ctx_row, tmm_c)
        fin = final_norm_g if l == depth - 1 else None
        xc = _ffn(xc, mods, g_ffn, w1, w2, l, ctx_row, tm_c, fin)

        if l == 0:
            proj, gz, xl = _inproj(xl, mods, g_mix, w_main, w_gate, l, lat_row(tm_l), tm_l, pos)
        else:
            proj, gz = _inproj(xl, mods, g_mix, w_main, w_gate, l, lat_row(tm_l), tm_l)
        og_f, og_b, oh_f, oh_b = _scan(proj, gz, mats, lrw_pad, lrb, lbs, state_gla, state_hgrn, l=l,
                                       n_seq=dec_batch, emit_final=False)
        xl = _merge(xl, og_f, og_b, oh_f, oh_b, proj, mods, params, l, lat_row(tmm_l), tmm_l)
        xl = _ffn(xl, mods, g_ffn, w1, w2, l, lat_row(tm_l), tm_l, fin)

    return (xc.reshape(batch, seq, d), xl.reshape(dec_batch, dec_seq, d), jnp.stack(new_gla, axis=1), jnp.stack(new_hg, axis=1))
```

```python
import functools

import jax
import jax.numpy as jnp
import numpy as np
from jax import lax
from jax.experimental import pallas as pl
from jax.experimental.pallas import tpu as pltpu

F32 = jnp.float32
BF16 = jnp.bfloat16

D_MODEL = 1024
GRID_W = 64
GLA_H, GLA_DK, GLA_DV, GLA_RANK = 4, 64, 128, 16
GLA_GATE_NORM = 16.0
HG_H, HG_DK, HG_DV = 4, 128, 128
GM_G, GM_CH, GM_CHUNK = 4, 128, 128
D_FF = 4 * D_MODEL
EPS = 1e-6
GLA_QK = GLA_H * GLA_DK
GLA_V = GLA_H * GLA_DV
HG_K = HG_H * HG_DK
HG_V = HG_H * HG_DV
GM_W = GM_G * GM_CH
SPLIT_SIZES = (GLA_QK, GLA_QK, GLA_V, GLA_V, GLA_RANK, GLA_RANK, HG_K, HG_K, HG_K, HG_V, HG_V, GM_W, GM_W,
               D_MODEL, D_MODEL, D_MODEL)
SPLIT_NAMES = ("gq", "gk", "gv", "go", "gzf", "gzb", "hq", "hff", "hfb", "hi", "ho", "mu", "mv",
               "a_gla", "a_hg", "a_gm")
MAIN_COLS = ("gq", "gk", "gv", "go", "hq", "hi", "ho", "mu", "mv", "a_gla", "a_hg", "a_gm")
GATE_COLS = ("hff", "hfb", "gzf", "gzb")
D_MAIN = 7168
D_GATE = 2 * HG_K + 128

G_GQK, G_GV, G_GO, G_HQ, G_HI, G_HO = range(6)
G_MUV = 3
G_AGLA, G_AHG, G_AGM = 4, 5, 6
G_HFF, G_HFB = 0, 1
G_GZ = 2 * HG_K // 128

LANE = 128
SCAN_C = 128
SCAN_STEP = 2
GROUP_K = 256
SUB = 16
N_SUB = SCAN_C // SUB
LEVELS = (4, 2, 1)
ROW_TILE = 1024
INPROJ_TN = 1792
MERGE_TILE = 512
VMEM_LIMIT = 56 * 1024 * 1024

HG_GROUPS = HG_K // GROUP_K
HG_GH = HG_H // HG_GROUPS
HG_GV = HG_GH * HG_DV


def _cparams(sem):
    return pltpu.CompilerParams(dimension_semantics=sem, vmem_limit_bytes=VMEM_LIMIT)


def _dot(a, b):
    return jnp.dot(a, b, preferred_element_type=F32)


def _dot_nt(a, b):
    return lax.dot_general(a, b, (((1,), (1,)), ((), ())), preferred_element_type=F32)


def _dot_tn(a, b):
    return lax.dot_general(a, b, (((0,), (0,)), ((), ())), preferred_element_type=F32)


def _sigmoid(x):
    return 0.5 + 0.5 * jnp.tanh(0.5 * x)


def _log_sigmoid(x):
    return jnp.minimum(x, 0.0) - jnp.log(1.0 + jnp.exp(-jnp.abs(x)))


def _gelu_tanh(x):
    c = np.sqrt(2.0 / np.pi).astype(np.float32)
    return x * (0.5 * (1.0 + jnp.tanh(c * (x + 0.044715 * (x * x * x)))))


def _mod_kernel(c_ref, w_ref, b_ref, o_ref):
    c = c_ref[...]
    s = (c * _sigmoid(c)).astype(BF16)
    o_ref[...] = _dot(s, w_ref[...].astype(BF16)) + b_ref[...]


def _modulations(cvec, ada_w, ada_b):
    depth, d, d6 = ada_w.shape
    r = cvec.shape[0]
    tn = 1024
    return pl.pallas_call(
        _mod_kernel,
        grid=(depth, d6 // tn),
        in_specs=[pl.BlockSpec((r, d), lambda l, j: (0, 0)),
                  pl.BlockSpec((None, d, tn), lambda l, j: (l, 0, j)),
                  pl.BlockSpec((None, 1, tn), lambda l, j: (l, 0, j))],
        out_specs=pl.BlockSpec((None, r, tn), lambda l, j: (l, 0, j)),
        out_shape=jax.ShapeDtypeStruct((depth, r, d6), F32),
        compiler_params=_cparams(("parallel", "parallel")),
        name="adaln_mod",
    )(cvec, ada_w, ada_b.reshape(depth, 1, d6))


def _lb_kernel(x_ref, o_ref):
    x = x_ref[...]
    m = jnp.max(x, axis=1, keepdims=True)
    e = jnp.exp(x - m)
    p = e / jnp.sum(e, axis=1, keepdims=True)
    acc = jnp.zeros_like(p[:, 0:1, :])
    o_ref[:, 0:1, :] = acc
    for l in range(1, x.shape[1]):
        acc = acc + p[:, l:l + 1, :]
        o_ref[:, l:l + 1, :] = acc


def _lower_bounds(logits):
    return pl.pallas_call(
        _lb_kernel,
        out_shape=jax.ShapeDtypeStruct(logits.shape, F32),
        name="hgrn_lower_bounds",
    )(logits)


def _regroup_kernel(w_ref, main_ref, gate_ref, *, main_ranges, gate_ranges):
    main_ref[...] = jnp.concatenate([w_ref[:, a:b] for a, b in main_ranges], axis=1).astype(BF16)
    pad = D_GATE - sum(b - a for a, b in gate_ranges)
    gate_ref[...] = jnp.concatenate([w_ref[:, a:b] for a, b in gate_ranges]
                                    + [jnp.zeros((w_ref.shape[0], pad), F32)], axis=1).astype(BF16)


def _regroup_w_in(w_in):
    depth, d, d_in = w_in.shape
    tr = 256
    return pl.pallas_call(
        functools.partial(_regroup_kernel, main_ranges=_col_ranges(MAIN_COLS), gate_ranges=_col_ranges(GATE_COLS)),
        grid=(depth, d // tr),
        in_specs=[pl.BlockSpec((None, tr, d_in), lambda l, i: (l, i, 0))],
        out_specs=[pl.BlockSpec((None, tr, D_MAIN), lambda l, i: (l, i, 0)),
                   pl.BlockSpec((None, tr, D_GATE), lambda l, i: (l, i, 0))],
        out_shape=[jax.ShapeDtypeStruct((depth, d, D_MAIN), BF16),
                   jax.ShapeDtypeStruct((depth, d, D_GATE), BF16)],
        compiler_params=_cparams(("parallel", "parallel")),
        name="regroup_w_in",
    )(w_in)


def _inproj_kernel(*refs, embed):
    if embed:
        x_ref, pos_ref, mod_ref, g_ref, w_ref, wgate_ref, out_ref, gate_ref, x0_ref, h_ref = refs
    else:
        x_ref, mod_ref, g_ref, w_ref, wgate_ref, out_ref, gate_ref, h_ref = refs

    @pl.when(pl.program_id(1) == 0)
    def _():
        x = x_ref[...]
        if embed:
            x = x + pos_ref[...]
            x0_ref[...] = x
        ms = jnp.mean(x * x, axis=-1, keepdims=True)
        hn = (x * lax.rsqrt(ms + EPS)) * g_ref[...]
        h = (hn * (1.0 + mod_ref[1:2, :]) + mod_ref[0:1, :]).astype(BF16)
        h_ref[...] = h
        gate_ref[...] = _dot(h, wgate_ref[...])

    out_ref[...] = _dot(h_ref[...], w_ref[...]).astype(out_ref.dtype)


def _inproj(x, mods, g, w_main, w_gate, l, mod_row, tm, pos=None):
    t, d = x.shape
    tn = INPROJ_TN
    embed = pos is not None
    pos_specs, pos_args, x0_specs, x0_shape = [], [], [], []
    if embed:
        tiles_per_seq = pos.shape[0] // tm
        pos_specs = [pl.BlockSpec((tm, d), lambda i, j: (i % tiles_per_seq, 0))]
        pos_args = [pos]
        x0_specs = [pl.BlockSpec((tm, d), lambda i, j: (i, 0))]
        x0_shape = [jax.ShapeDtypeStruct((t, d), F32)]
    return pl.pallas_call(
        functools.partial(_inproj_kernel, embed=embed),
        grid=(t // tm, D_MAIN // tn),
        in_specs=[pl.BlockSpec((tm, d), lambda i, j: (i, 0))] + pos_specs + [
                  pl.BlockSpec((None, None, 6, d), lambda i, j: (l, mod_row(i), 0, 0)),
                  pl.BlockSpec((None, 1, d), lambda i, j: (l, 0, 0)),
                  pl.BlockSpec((None, d, tn), lambda i, j: (l, 0, j)),
                  pl.BlockSpec((None, d, D_GATE), lambda i, j: (l, 0, 0))],
        out_specs=[pl.BlockSpec((tm, tn), lambda i, j: (i, j)),
                   pl.BlockSpec((tm, D_GATE), lambda i, j: (i, 0))] + x0_specs,
        out_shape=[jax.ShapeDtypeStruct((t, D_MAIN), BF16),
                   jax.ShapeDtypeStruct((t, D_GATE), F32)] + x0_shape,
        scratch_shapes=[pltpu.VMEM((tm, d), BF16)],
        compiler_params=_cparams(("parallel", "arbitrary")),
        name="inproj",
    )(x, *pos_args, mods, g, w_main, w_gate)


R_LOCAL = 0
R_BLOCK = SCAN_C
R_PRE = R_BLOCK + N_SUB
R_POST = R_PRE + N_SUB
R_LEVEL = R_POST + N_SUB
R_TOTAL = R_LEVEL + len(LEVELS) * N_SUB
R_ROWS = -(-(R_TOTAL + N_SUB) // 16) * 16


def _scan_matrices():
    pos = np.arange(SCAN_C)
    blk = pos // SUB
    out = np.zeros((2, R_ROWS, SCAN_C), np.float32)
    for d, fwd in enumerate((True, False)):
        order = np.arange(N_SUB) if fwd else N_SUB - 1 - np.arange(N_SUB)
        earlier_eq = (pos[None, :] <= pos[:, None]) if fwd else (pos[None, :] >= pos[:, None])
        out[d, R_LOCAL:R_LOCAL + SCAN_C] = (blk[:, None] == blk[None, :]) & earlier_eq
        um = order[:, None]
        un = order[None, :]

        def put(row0, cond):
            out[d, row0:row0 + N_SUB] = cond[:, blk]

        put(R_BLOCK, um == un)
        put(R_PRE, un < um)
        put(R_POST, un > um)
        for i, bb in enumerate(LEVELS):
            sb = 2 * bb
            w, wn = um % sb, un % sb
            q_side = w >= bb
            inside = np.where(q_side, (wn >= bb) & (wn < w), (wn > w) & (wn < bb))
            put(R_LEVEL + i * N_SUB, ((um // sb) == (un // sb)) & inside)
        put(R_TOTAL, np.ones((N_SUB, N_SUB), bool))
    return jnp.asarray(out, BF16)


def _per_block(x8):
    return jnp.concatenate([jnp.broadcast_to(x8[m:m + 1, :], (SUB, x8.shape[1])) for m in range(N_SUB)], axis=0)


def _div_p2(x, n):
    return lax.shift_right_logical(x, int(n).bit_length() - 1)


def _mod_p2(x, n):
    return lax.bitwise_and(x, n - 1)


class _Problem:
    def __init__(self, fwd, hg, dv, q, k, la, v, prev, s_ref, s_idx, o_ref, o_rows, o_cols):
        self.fwd, self.hg, self.dv = fwd, hg, dv
        self.q, self.k, self.la, self.v = q, k, la, v
        self.prev = prev
        self.s_ref, self.s_idx, self.o_ref, self.o_rows, self.o_cols = s_ref, s_idx, o_ref, o_rows, o_cols


def _stack_heads(b, hg):
    c = b.shape[0]
    dk = GROUP_K // hg
    per_lane = LANE // dk
    zb = jnp.zeros((c, LANE), BF16)
    blocks = []
    for h in range(hg):
        lv = (h * dk) // LANE
        piece = b[:, lv * LANE:(lv + 1) * LANE]
        if per_lane > 1:
            sub_head = _div_p2(lax.broadcasted_iota(jnp.int32, (c, LANE), 1), dk)
            piece = jnp.where(sub_head == (h % per_lane), piece, 0.0)
        piece = piece.astype(BF16)
        blocks.append(jnp.concatenate([piece if j == lv else zb for j in range(GROUP_K // LANE)], axis=1))
    return jnp.concatenate(blocks, axis=0)


def _stage_factors(p, r):
    c = SCAN_C
    local = r[R_LOCAL:R_LOCAL + c]
    block_total = _per_block(r[R_BLOCK:R_BLOCK + N_SUB])
    small = jnp.exp(r[R_PRE:R_TOTAL])
    pre = _per_block(small[0:N_SUB])
    post = _per_block(small[N_SUB:2 * N_SUB])
    ql = p.q * jnp.exp(local)
    kr = p.k * jnp.exp(block_total - local)
    ki = p.k * jnp.exp(-local)
    p.q_t = (ql * pre).astype(BF16)
    p.k_t = (kr * post).astype(BF16)
    ql_b = ql.astype(BF16)
    masked = GROUP_K // p.hg < LANE
    kr_b = kr if masked else kr.astype(BF16)
    p.a_ops = [ql_b]
    p.b_ops = [_stack_heads(ki, p.hg)]
    for i in range(len(LEVELS)):
        f = _per_block(small[(2 + i) * N_SUB:(3 + i) * N_SUB])
        f_b = f.astype(BF16)
        p.a_ops.append(ql_b * f_b)
        p.b_ops.append(_stack_heads(kr_b * (f if masked else f_b), p.hg))
    tot_rows = jnp.concatenate([r[R_TOTAL:R_TOTAL + N_SUB]] * (GROUP_K // N_SUB), axis=0)
    eye = (lax.broadcasted_iota(jnp.int32, (GROUP_K, GROUP_K), 0)
           == lax.broadcasted_iota(jnp.int32, (GROUP_K, GROUP_K), 1))
    p.decay_col = jnp.exp(jnp.sum(jnp.where(eye, tot_rows, 0.0), axis=1, keepdims=True))


def _stage_products(p):
    hg, dv = p.hg, p.dv
    dk = GROUP_K // hg
    s_in = p.s_ref[p.s_idx] if p.prev is None else p.prev.s_out
    p.o_inter = _dot(p.q_t, s_in.astype(BF16))
    kv = _dot_tn(p.k_t, p.v)
    kv_bd = _blockdiag([kv[h * dk:(h + 1) * dk, h * dv:(h + 1) * dv] for h in range(hg)])
    p.s_out = s_in * p.decay_col + kv_bd
    p.prods = [_dot_nt(a, b) for a, b in zip(p.a_ops, p.b_ops)]


def _stage_finish(p):
    c, hg, dv = SCAN_C, p.hg, p.dv
    si = lax.broadcasted_iota(jnp.int32, (c, hg * c), 0)
    sj = _mod_p2(lax.broadcasted_iota(jnp.int32, (c, hg * c), 1), c)
    m = (_div_p2(si, SUB) == _div_p2(sj, SUB)) & ((sj <= si) if p.fwd else (sj >= si))
    scores = jnp.where(m, p.prods[0], 0.0)
    for i, bb in enumerate(LEVELS):
        blk = bb * SUB
        sb = 2 * blk
        i_second = _mod_p2(si, sb) >= blk
        j_second = _mod_p2(sj, sb) >= blk
        same = _div_p2(si, sb) == _div_p2(sj, sb)
        if p.fwd:
            m = same & i_second & jnp.logical_not(j_second)
        else:
            m = same & jnp.logical_not(i_second) & j_second
        scores = jnp.where(m, p.prods[1 + i], scores)

    zv = jnp.zeros((c, dv), BF16)
    v_bd = jnp.concatenate(
        [jnp.concatenate([p.v[:, h * dv:(h + 1) * dv] if j == h else zv for j in range(hg)], axis=1)
         for h in range(hg)], axis=0)
    o = p.o_inter + _dot(scores.astype(BF16), v_bd)
    p.o_ref[p.o_rows, p.o_cols] = o.astype(p.o_ref.dtype)


def _blockdiag(blocks):
    hg = len(blocks)
    z = jnp.zeros_like(blocks[0])
    return jnp.concatenate(
        [jnp.concatenate([blocks[h] if j == h else z for j in range(hg)], axis=1) for h in range(hg)], axis=0)


def _scan_kernel(*refs, has_init, emit_final, n_sub):
    tiles = (refs[0:6], refs[6:12])
    mat_ref, lrw_ref, lrb_ref, lb_ref = refs[12:16]
    pos = 16
    if has_init:
        sg0_ref, sh0_ref = refs[pos:pos + 2]
        pos += 2
    og_refs = refs[pos:pos + 2]
    oh_refs = refs[pos + 2:pos + 4]
    pos += 4
    if emit_final:
        ng_ref, nh_ref = refs[pos:pos + 2]
        pos += 2
    sg_ref, sh_ref = refs[pos:pos + 2]

    ci = pl.program_id(1)

    @pl.when(ci == 0)
    def _():
        if has_init:
            for d in range(2):
                sg_ref[d] = _blockdiag([sg0_ref[d, h] for h in range(GLA_H)])
                for g in range(HG_GROUPS):
                    sh_ref[d, g] = _blockdiag([sh0_ref[d, g * HG_GH + h] for h in range(HG_GH)])
        else:
            sg_ref[...] = jnp.zeros_like(sg_ref)
            sh_ref[...] = jnp.zeros_like(sh_ref)

    waves = [[] for _ in range(n_sub)]
    for d in range(2):
        fwd = d == 0
        gqk_ref, gv_ref, gz_ref, hq_ref, hf_ref, hi_ref = tiles[d]
        prev = [None] * (1 + HG_GROUPS)
        for n in range(n_sub):
            sub = n if fwd else n_sub - 1 - n
            rows = slice(sub * SCAN_C, (sub + 1) * SCAN_C)
            z = _dot(gz_ref[rows, :].astype(BF16), lrw_ref[d]) + lrb_ref[d]
            group = [_Problem(fwd, GLA_H, GLA_DV,
                              gqk_ref[rows, :GLA_QK].astype(F32) * (GLA_DK ** -0.5),
                              gqk_ref[rows, GLA_QK:].astype(F32),
                              _log_sigmoid(z) * (1.0 / GLA_GATE_NORM), gv_ref[rows, :],
                              prev[0], sg_ref, (d,), og_refs[d], rows, slice(0, GLA_V))]
            for g in range(HG_GROUPS):
                ks = slice(g * GROUP_K, (g + 1) * GROUP_K)
                vs = slice(g * HG_GV, (g + 1) * HG_GV)
                lb = lb_ref[d, :, ks]
                sig = 1.0 / (1.0 + jnp.exp(-hf_ref[rows, ks]))
                group.append(_Problem(fwd, HG_GH, HG_DV,
                                      hq_ref[rows, ks].astype(F32) * (HG_DK ** -0.5), (1.0 - lb) * (1.0 - sig),
                                      jnp.log(lb + (1.0 - lb) * sig), hi_ref[rows, vs],
                                      prev[1 + g], sh_ref, (d, g), oh_refs[d], rows, vs))
            la = jnp.concatenate([p.la for p in group], axis=1)
            la_hi = la.astype(BF16)
            la_lo = (la - la_hi.astype(F32)).astype(BF16)
            r = _dot(mat_ref[d], la_hi) + _dot(mat_ref[d], la_lo)
            for i, p in enumerate(group):
                p.r = r[:, i * GROUP_K:(i + 1) * GROUP_K]
            waves[n] += group
            prev = group

    probs = [p for wave in waves for p in wave]
    for p in probs:
        _stage_factors(p, p.r)
    for p in probs:
        _stage_products(p)
    for p in probs:
        _stage_finish(p)
    last_wave = waves[-1]
    for p in last_wave:
        p.s_ref[p.s_idx] = p.s_out

    if emit_final:
        @pl.when(ci == pl.num_programs(1) - 1)
        def _():
            for d in range(2):
                s = last_wave[3 * d].s_out
                for h in range(GLA_H):
                    ng_ref[d, h] = s[h * GLA_DK:(h + 1) * GLA_DK, h * GLA_DV:(h + 1) * GLA_DV]
                for g in range(HG_GROUPS):
                    s = last_wave[3 * d + 1 + g].s_out
                    for h in range(HG_GH):
                        nh_ref[d, g * HG_GH + h] = s[h * HG_DK:(h + 1) * HG_DK, h * HG_DV:(h + 1) * HG_DV]


def _scan(proj, gates, mats, lrw_pad, lrb, lbs, s0_gla, s0_hg, *, l, n_seq, emit_final):
    t = proj.shape[0]
    n_sub = min(SCAN_STEP, t // n_seq // SCAN_C)
    rows = n_sub * SCAN_C
    nc = t // n_seq // rows
    has_init = s0_gla is not None

    def row(d):
        if d == 0:
            return lambda s, c: s * nc + c
        return lambda s, c: s * nc + (nc - 1 - c)

    def col(r, g):
        return lambda s, c: (r(s, c), g)

    in_specs, args = [], []
    for d in range(2):
        r = row(d)
        in_specs += [pl.BlockSpec((rows, 512), col(r, G_GQK)),
                     pl.BlockSpec((rows, 512), col(r, G_GV)),
                     pl.BlockSpec((rows, LANE), col(r, G_GZ)),
                     pl.BlockSpec((rows, 512), col(r, G_HQ)),
                     pl.BlockSpec((rows, 512), col(r, G_HFF if d == 0 else G_HFB)),
                     pl.BlockSpec((rows, 512), col(r, G_HI))]
        args += [proj, proj, gates, proj, gates, proj]
    in_specs += [pl.BlockSpec((2, R_ROWS, SCAN_C), lambda s, c: (0, 0, 0)),
                 pl.BlockSpec((None, 2, LANE, GLA_QK), lambda s, c: (l, 0, 0, 0)),
                 pl.BlockSpec((None, 2, 1, GLA_QK), lambda s, c: (l, 0, 0, 0)),
                 pl.BlockSpec((2, None, 1, HG_K), lambda s, c: (0, l, 0, 0))]
    args += [mats, lrw_pad, lrb, lbs]
    if has_init:
        in_specs += [pl.BlockSpec((None, None, 2, GLA_H, GLA_DK, GLA_DV), lambda s, c: (s, l, 0, 0, 0, 0)),
                     pl.BlockSpec((None, None, 2, HG_H, HG_DK, HG_DV), lambda s, c: (s, l, 0, 0, 0, 0))]
        args += [s0_gla, s0_hg]
    out_specs = [pl.BlockSpec((rows, 512), col(row(d), 0)) for d in (0, 1, 0, 1)]
    out_shape = [jax.ShapeDtypeStruct((t, 512), BF16)] * 4
    if emit_final:
        out_specs += [pl.BlockSpec((None, 2, GLA_H, GLA_DK, GLA_DV), lambda s, c: (s, 0, 0, 0, 0)),
                      pl.BlockSpec((None, 2, HG_H, HG_DK, HG_DV), lambda s, c: (s, 0, 0, 0, 0))]
        out_shape += [jax.ShapeDtypeStruct((n_seq, 2, GLA_H, GLA_DK, GLA_DV), F32),
                      jax.ShapeDtypeStruct((n_seq, 2, HG_H, HG_DK, HG_DV), F32)]
    return pl.pallas_call(
        functools.partial(_scan_kernel, has_init=has_init, emit_final=emit_final, n_sub=n_sub),
        grid=(n_seq, nc),
        in_specs=in_specs,
        out_specs=out_specs,
        out_shape=out_shape,
        scratch_shapes=[pltpu.VMEM((2, GROUP_K, GLA_V), F32),
                        pltpu.VMEM((2, HG_GROUPS, GROUP_K, HG_GV), F32)],
        compiler_params=_cparams(("parallel", "arbitrary")),
        name="scan",
    )(*args)


def _head_norm_gate(o_fwd, o_bwd, g, gate, n_heads, dv):
    o = (o_fwd + o_bwd).astype(F32)
    half = 0.5 * gate
    silu = (half + half * jnp.tanh(half)).astype(F32)
    parts = []
    for h in range(n_heads):
        oh = o[:, h * dv:(h + 1) * dv]
        parts.append(oh * lax.rsqrt(jnp.mean(oh * oh, axis=-1, keepdims=True) + EPS))
    return jnp.concatenate(parts, axis=1) * g * silu


def _merge_kernel(x_ref, ogf_ref, ogb_ref, ohf_ref, ohb_ref, go_ref, ho_ref, muv_ref, agla_ref, ahg_ref, agm_ref,
                  mod_ref, gng_ref, hng_ref, gmg_ref, ws_ref, bias_ref, wbg_ref, wbh_ref, wbm_ref, wout_ref,
                  out_ref):
    o_gla = _head_norm_gate(ogf_ref[...], ogb_ref[...], gng_ref[...], go_ref[...], GLA_H, GLA_DV)
    o_hg = _head_norm_gate(ohf_ref[...], ohb_ref[...], hng_ref[...], ho_ref[...], HG_H, HG_DV)

    u = _gelu_tanh(muv_ref[:, :GM_W].astype(F32))
    v = _gelu_tanh(muv_ref[:, GM_W:].astype(F32))
    mu = jnp.mean(v, axis=-1, keepdims=True)
    vc = v - mu
    var = jnp.mean(vc * vc, axis=-1, keepdims=True)
    vn = ((vc * lax.rsqrt(var + EPS)) * gmg_ref[...]).astype(BF16)
    rows = []
    for n in range(vn.shape[0] // GM_CHUNK):
        cols = [_dot(ws_ref[g], vn[n * GM_CHUNK:(n + 1) * GM_CHUNK, g * GM_CH:(g + 1) * GM_CH]) for g in range(GM_G)]
        rows.append(jnp.concatenate(cols, axis=1) + bias_ref[...])
    mixed = rows[0] if len(rows) == 1 else jnp.concatenate(rows, axis=0)
    o_gm = u * mixed

    merged = (_sigmoid(agla_ref[...]) * _dot(o_gla.astype(BF16), wbg_ref[...]).astype(BF16)
              + _sigmoid(ahg_ref[...]) * _dot(o_hg.astype(BF16), wbh_ref[...]).astype(BF16)
              + _sigmoid(agm_ref[...]) * _dot(o_gm.astype(BF16), wbm_ref[...]).astype(BF16))
    mix = _dot(merged, wout_ref[...])
    out_ref[...] = x_ref[...] + mod_ref[2:3, :] * mix


def _merge(x, og_f, og_b, oh_f, oh_b, proj, mods, p, l, mod_row, tm):
    t, d = x.shape
    row_spec = lambda w, col: pl.BlockSpec((tm, w), lambda i: (i, col))
    lay = lambda *shape: pl.BlockSpec((None,) + shape, lambda i: (l,) + (0,) * len(shape))
    return pl.pallas_call(
        _merge_kernel,
        grid=(t // tm,),
        in_specs=[row_spec(d, 0),
                  row_spec(GLA_V, 0), row_spec(GLA_V, 0), row_spec(HG_V, 0), row_spec(HG_V, 0),
                  row_spec(512, G_GO), row_spec(512, G_HO), row_spec(1024, G_MUV),
                  row_spec(1024, G_AGLA), row_spec(1024, G_AHG), row_spec(1024, G_AGM),
                  pl.BlockSpec((None, None, 6, d), lambda i: (l, mod_row(i), 0, 0)),
                  lay(1, GLA_V), lay(1, HG_V), lay(1, GM_W),
                  lay(GM_G, GM_CHUNK, GM_CHUNK), lay(GM_CHUNK, GM_W),
                  lay(GLA_V, d), lay(HG_V, d), lay(GM_W, d), lay(d, d)],
        out_specs=pl.BlockSpec((tm, d), lambda i: (i, 0)),
        out_shape=jax.ShapeDtypeStruct((t, d), F32),
        compiler_params=_cparams(("parallel",)),
        name="merge",
    )(x, og_f, og_b, oh_f, oh_b, proj, proj, proj, proj, proj, proj, mods,
      p["gla_norm_g"], p["hg_norm_g"], p["gm_norm_g"], p["gm_ws"], p["gm_bias"],
      p["w_br_gla"], p["w_br_hg"], p["w_br_gm"], p["w_out"])


def _ffn_kernel(x_ref, mod_ref, g_ref, w1_ref, w2_ref, *rest, final):
    if final:
        gfin_ref, out_ref, h_ref, acc_ref = rest
    else:
        out_ref, h_ref, acc_ref = rest
    j = pl.program_id(1)

    @pl.when(j == 0)
    def _():
        x = x_ref[...]
        ms = jnp.mean(x * x, axis=-1, keepdims=True)
        hn = (x * lax.rsqrt(ms + EPS)) * g_ref[...]
        h_ref[...] = (hn * (1.0 + mod_ref[4:5, :]) + mod_ref[3:4, :]).astype(BF16)
        acc_ref[...] = jnp.zeros_like(acc_ref)

    a = jnp.maximum(_dot(h_ref[...], w1_ref[...]), 0.0)
    acc_ref[...] += _dot((a * a).astype(BF16), w2_ref[...])

    @pl.when(j == pl.num_programs(1) - 1)
    def _():
        y = x_ref[...] + mod_ref[5:6, :] * acc_ref[...]
        if final:
            y = (y * lax.rsqrt(jnp.mean(y * y, axis=-1, keepdims=True) + EPS)) * gfin_ref[...]
        out_ref[...] = y


def _ffn(x, mods, g, w1, w2, l, mod_row, tm, final_g=None):
    t, d = x.shape
    tf = 1024
    final = final_g is not None
    extra_specs = [pl.BlockSpec((1, d), lambda i, j: (0, 0))] if final else []
    extra_args = [final_g.reshape(1, d)] if final else []
    return pl.pallas_call(
        functools.partial(_ffn_kernel, final=final),
        grid=(t // tm, D_FF // tf),
        in_specs=[pl.BlockSpec((tm, d), lambda i, j: (i, 0)),
                  pl.BlockSpec((None, None, 6, d), lambda i, j: (l, mod_row(i), 0, 0)),
                  pl.BlockSpec((None, 1, d), lambda i, j: (l, 0, 0)),
                  pl.BlockSpec((None, d, tf), lambda i, j: (l, 0, j)),
                  pl.BlockSpec((None, tf, d), lambda i, j: (l, j, 0))] + extra_specs,
        out_specs=pl.BlockSpec((tm, d), lambda i, j: (i, 0)),
        out_shape=jax.ShapeDtypeStruct((t, d), F32),
        scratch_shapes=[pltpu.VMEM((tm, d), BF16), pltpu.VMEM((tm, d), F32)],
        compiler_params=_cparams(("parallel", "arbitrary")),
        name="ffn",
    )(x, mods, g, w1, w2, *extra_args)


def _grid_posemb(n_tok):
    rows = n_tok // GRID_W
    r, col = jnp.meshgrid(jnp.arange(rows, dtype=F32), jnp.arange(GRID_W, dtype=F32), indexing="ij")
    r = r.reshape(-1)
    col = col.reshape(-1)
    nf = D_MODEL // 4
    omega = 1.0 / (10000.0 ** (jnp.arange(nf, dtype=F32) / nf))
    er = r[:, None] * omega
    ec = col[:, None] * omega
    return jnp.concatenate([jnp.sin(er), jnp.cos(er), jnp.sin(ec), jnp.cos(ec)], axis=-1)


def _col_ranges(names):
    starts = dict(zip(SPLIT_NAMES, np.cumsum((0,) + SPLIT_SIZES[:-1])))
    sizes = dict(zip(SPLIT_NAMES, SPLIT_SIZES))
    ranges = []
    for n in names:
        a, b = int(starts[n]), int(starts[n]) + sizes[n]
        if ranges and ranges[-1][1] == a:
            ranges[-1] = (ranges[-1][0], b)
        else:
            ranges.append((a, b))
    return ranges


def _tile_rows(n_rows, seq_rows, shared_mod):
    return min(ROW_TILE, n_rows if shared_mod else seq_rows)


def kernel(x_prompt, x_sample, c, state_gla, state_hgrn, c_ctx, ada_w, ada_b, norm_mix_g, norm_ffn_g, w_in, gla_lr_w,
           gla_lr_b, gla_norm_g, hg_lb_logits, hg_norm_g, gm_norm_g, gm_ws, gm_bs, w_br_gla, w_br_hg, w_br_gm, w_out,
           w_ff1, w_ff2, final_norm_g):
    batch, seq, d = x_prompt.shape
    dec_batch, dec_seq, _ = x_sample.shape
    depth = ada_w.shape[0]

    w_main, w_gate = _regroup_w_in(w_in)
    lrw_pad = jnp.stack(
        [jnp.pad(gla_lr_w[:, dd], ((0, 0), (dd * GLA_RANK, LANE - (dd + 1) * GLA_RANK), (0, 0))) for dd in range(2)],
        axis=1).astype(BF16)
    lrb = gla_lr_b.reshape(depth, 2, 1, GLA_QK)
    gm_bias = jnp.broadcast_to(jnp.swapaxes(gm_bs, 1, 2)[:, :, :, None],
                               (depth, GM_CHUNK, GM_G, GM_CH)).reshape(depth, GM_CHUNK, GM_W)
    params = {
        "gla_norm_g": gla_norm_g.reshape(depth, 1, GLA_V),
        "hg_norm_g": hg_norm_g.reshape(depth, 1, HG_V),
        "gm_norm_g": gm_norm_g.reshape(depth, 1, GM_W),
        "gm_ws": gm_ws.astype(BF16),
        "gm_bias": gm_bias,
        "w_br_gla": w_br_gla.astype(BF16),
        "w_br_hg": w_br_hg.astype(BF16),
        "w_br_gm": w_br_gm.astype(BF16),
        "w_out": w_out.astype(BF16),
    }
    w1 = w_ff1.astype(BF16)
    w2 = w_ff2.astype(BF16)
    g_mix = norm_mix_g.reshape(depth, 1, d)
    g_ffn = norm_ffn_g.reshape(depth, 1, d)
    mats = _scan_matrices()

    n_mod = 1 + dec_batch
    n_mod_pad = -(-n_mod // 8) * 8
    cvec = jnp.concatenate([c_ctx[None, :], c, jnp.zeros((n_mod_pad - n_mod, d), F32)], axis=0)
    mods = _modulations(cvec, ada_w, ada_b).reshape(depth, n_mod_pad, 6, d)
    lbs = _lower_bounds(hg_lb_logits).reshape(2, depth, 1, HG_K)

    xc = x_prompt.reshape(batch * seq, d)
    xl = x_sample.reshape(dec_batch * dec_seq, d)
    pos = _grid_posemb(dec_seq)

    tm_c = _tile_rows(batch * seq, seq, True)
    tm_l = _tile_rows(dec_batch * dec_seq, dec_seq, False)
    tmm_c = min(MERGE_TILE, batch * seq)
    tmm_l = min(MERGE_TILE, dec_seq)
    ctx_row = lambda i: 0
    lat_row = lambda tm: (lambda i: 1 + (i * tm) // dec_seq)
    new_gla = []
    new_hg = []
    for l in range(depth):
        proj, gz = _inproj(xc, mods, g_mix, w_main, w_gate, l, ctx_row, tm_c)
        og_f, og_b, oh_f, oh_b, ng, nh = _scan(proj, gz, mats, lrw_pad, lrb, lbs, None, None, l=l, n_seq=batch,
                                               emit_final=True)
        new_gla.append(ng)
        new_hg.append(nh)
        xc = _merge(xc, og_f, og_b, oh_f, oh_b, proj, mods, params, l, ctx_row, tmm_c)
        fin = final_norm_g if l == depth - 1 else None
        xc = _ffn(xc, mods, g_ffn, w1, w2, l, ctx_row, tm_c, fin)

        if l == 0:
            proj, gz, xl = _inproj(xl, mods, g_mix, w_main, w_gate, l, lat_row(tm_l), tm_l, pos)
        else:
            proj, gz = _inproj(xl, mods, g_mix, w_main, w_gate, l, lat_row(tm_l), tm_l)
        og_f, og_b, oh_f, oh_b = _scan(proj, gz, mats, lrw_pad, lrb, lbs, state_gla, state_hgrn, l=l,
                                       n_seq=dec_batch, emit_final=False)
        xl = _merge(xl, og_f, og_b, oh_f, oh_b, proj, mods, params, l, lat_row(tmm_l), tmm_l)
        xl = _ffn(xl, mods, g_ffn, w1, w2, l, lat_row(tm_l), tm_l, fin)

    return (xc.reshape(batch, seq, d), xl.reshape(dec_batch, dec_seq, d), jnp.stack(new_gla, axis=1), jnp.stack(new_hg, axis=1))
```

```python
import functools

import jax
import jax.numpy as jnp
import numpy as np
from jax import lax
from jax.experimental import pallas as pl
from jax.experimental.pallas import tpu as pltpu

F32 = jnp.float32
BF16 = jnp.bfloat16

D_MODEL = 1024
GRID_W = 64
GLA_H, GLA_DK, GLA_DV, GLA_RANK = 4, 64, 128, 16
GLA_GATE_NORM = 16.0
HG_H, HG_DK, HG_DV = 4, 128, 128
GM_G, GM_CH, GM_CHUNK = 4, 128, 128
D_FF = 4 * D_MODEL
EPS = 1e-6
GLA_QK = GLA_H * GLA_DK
GLA_V = GLA_H * GLA_DV
HG_K = HG_H * HG_DK
HG_V = HG_H * HG_DV
GM_W = GM_G * GM_CH
SPLIT_SIZES = (GLA_QK, GLA_QK, GLA_V, GLA_V, GLA_RANK, GLA_RANK, HG_K, HG_K, HG_K, HG_V, HG_V, GM_W, GM_W,
               D_MODEL, D_MODEL, D_MODEL)
SPLIT_NAMES = ("gq", "gk", "gv", "go", "gzf", "gzb", "hq", "hff", "hfb", "hi", "ho", "mu", "mv",
               "a_gla", "a_hg", "a_gm")
MAIN_COLS = ("gq", "gk", "gv", "go", "hq", "hi", "ho", "mu", "mv", "a_gla", "a_hg", "a_gm")
GATE_COLS = ("hff", "hfb", "gzf", "gzb")
D_MAIN = 7168
D_GATE = 2 * HG_K + 128

G_GQK, G_GV, G_GO, G_HQ, G_HI, G_HO = range(6)
G_MUV = 3
G_AGLA, G_AHG, G_AGM = 4, 5, 6
G_HFF, G_HFB = 0, 1
G_GZ = 2 * HG_K // 128

LANE = 128
SCAN_C = 128
SCAN_STEP = 4
GROUP_K = 256
SUB = 16
N_SUB = SCAN_C // SUB
LEVELS = (4, 2, 1)
ROW_TILE = 1024
INPROJ_TN = 1792
MERGE_TILE = 512
VMEM_LIMIT = 56 * 1024 * 1024

HG_GROUPS = HG_K // GROUP_K
HG_GH = HG_H // HG_GROUPS
HG_GV = HG_GH * HG_DV


def _cparams(sem):
    return pltpu.CompilerParams(dimension_semantics=sem, vmem_limit_bytes=VMEM_LIMIT)


def _dot(a, b):
    return jnp.dot(a, b, preferred_element_type=F32)


def _dot_nt(a, b):
    return lax.dot_general(a, b, (((1,), (1,)), ((), ())), preferred_element_type=F32)


def _dot_tn(a, b):
    return lax.dot_general(a, b, (((0,), (0,)), ((), ())), preferred_element_type=F32)


def _sigmoid(x):
    return 0.5 + 0.5 * jnp.tanh(0.5 * x)


def _log_sigmoid(x):
    return jnp.minimum(x, 0.0) - jnp.log(1.0 + jnp.exp(-jnp.abs(x)))


def _gelu_tanh(x):
    c = np.sqrt(2.0 / np.pi).astype(np.float32)
    return x * (0.5 * (1.0 + jnp.tanh(c * (x + 0.044715 * (x * x * x)))))


def _mod_kernel(c_ref, w_ref, b_ref, o_ref):
    c = c_ref[...]
    s = (c * _sigmoid(c)).astype(BF16)
    o_ref[...] = _dot(s, w_ref[...].astype(BF16)) + b_ref[...]


def _modulations(cvec, ada_w, ada_b):
    depth, d, d6 = ada_w.shape
    r = cvec.shape[0]
    tn = 1024
    return pl.pallas_call(
        _mod_kernel,
        grid=(depth, d6 // tn),
        in_specs=[pl.BlockSpec((r, d), lambda l, j: (0, 0)),
                  pl.BlockSpec((None, d, tn), lambda l, j: (l, 0, j)),
                  pl.BlockSpec((None, 1, tn), lambda l, j: (l, 0, j))],
        out_specs=pl.BlockSpec((None, r, tn), lambda l, j: (l, 0, j)),
        out_shape=jax.ShapeDtypeStruct((depth, r, d6), F32),
        compiler_params=_cparams(("parallel", "parallel")),
        name="adaln_mod",
    )(cvec, ada_w, ada_b.reshape(depth, 1, d6))


def _lb_kernel(x_ref, o_ref):
    x = x_ref[...]
    m = jnp.max(x, axis=1, keepdims=True)
    e = jnp.exp(x - m)
    p = e / jnp.sum(e, axis=1, keepdims=True)
    acc = jnp.zeros_like(p[:, 0:1, :])
    o_ref[:, 0:1, :] = acc
    for l in range(1, x.shape[1]):
        acc = acc + p[:, l:l + 1, :]
        o_ref[:, l:l + 1, :] = acc


def _lower_bounds(logits):
    return pl.pallas_call(
        _lb_kernel,
        out_shape=jax.ShapeDtypeStruct(logits.shape, F32),
        name="hgrn_lower_bounds",
    )(logits)


def _regroup_kernel(w_ref, main_ref, gate_ref, *, main_ranges, gate_ranges):
    main_ref[...] = jnp.concatenate([w_ref[:, a:b] for a, b in main_ranges], axis=1).astype(BF16)
    pad = D_GATE - sum(b - a for a, b in gate_ranges)
    gate_ref[...] = jnp.concatenate([w_ref[:, a:b] for a, b in gate_ranges]
                                    + [jnp.zeros((w_ref.shape[0], pad), F32)], axis=1).astype(BF16)


def _regroup_w_in(w_in):
    depth, d, d_in = w_in.shape
    tr = 256
    return pl.pallas_call(
        functools.partial(_regroup_kernel, main_ranges=_col_ranges(MAIN_COLS), gate_ranges=_col_ranges(GATE_COLS)),
        grid=(depth, d // tr),
        in_specs=[pl.BlockSpec((None, tr, d_in), lambda l, i: (l, i, 0))],
        out_specs=[pl.BlockSpec((None, tr, D_MAIN), lambda l, i: (l, i, 0)),
                   pl.BlockSpec((None, tr, D_GATE), lambda l, i: (l, i, 0))],
        out_shape=[jax.ShapeDtypeStruct((depth, d, D_MAIN), BF16),
                   jax.ShapeDtypeStruct((depth, d, D_GATE), BF16)],
        compiler_params=_cparams(("parallel", "parallel")),
        name="regroup_w_in",
    )(w_in)


def _inproj_kernel(*refs, embed):
    if embed:
        x_ref, pos_ref, mod_ref, g_ref, w_ref, wgate_ref, out_ref, gate_ref, x0_ref, h_ref = refs
    else:
        x_ref, mod_ref, g_ref, w_ref, wgate_ref, out_ref, gate_ref, h_ref = refs

    @pl.when(pl.program_id(1) == 0)
    def _():
        x = x_ref[...]
        if embed:
            x = x + pos_ref[...]
            x0_ref[...] = x
        ms = jnp.mean(x * x, axis=-1, keepdims=True)
        hn = (x * lax.rsqrt(ms + EPS)) * g_ref[...]
        h_ref[...] = (hn * (1.0 + mod_ref[1:2, :]) + mod_ref[0:1, :]).astype(BF16)

    @pl.when(pl.program_id(1) == 1)
    def _():
        gate_ref[...] = _dot(h_ref[...], wgate_ref[...])

    out_ref[...] = _dot(h_ref[...], w_ref[...]).astype(out_ref.dtype)


def _inproj(x, mods, g, w_main, w_gate, l, mod_row, tm, pos=None):
    t, d = x.shape
    tn = INPROJ_TN
    embed = pos is not None
    pos_specs, pos_args, x0_specs, x0_shape = [], [], [], []
    if embed:
        tiles_per_seq = pos.shape[0] // tm
        pos_specs = [pl.BlockSpec((tm, d), lambda i, j: (i % tiles_per_seq, 0))]
        pos_args = [pos]
        x0_specs = [pl.BlockSpec((tm, d), lambda i, j: (i, 0))]
        x0_shape = [jax.ShapeDtypeStruct((t, d), F32)]
    return pl.pallas_call(
        functools.partial(_inproj_kernel, embed=embed),
        grid=(t // tm, D_MAIN // tn),
        in_specs=[pl.BlockSpec((tm, d), lambda i, j: (i, 0))] + pos_specs + [
                  pl.BlockSpec((None, None, 6, d), lambda i, j: (l, mod_row(i), 0, 0)),
                  pl.BlockSpec((None, 1, d), lambda i, j: (l, 0, 0)),
                  pl.BlockSpec((None, d, tn), lambda i, j: (l, 0, j)),
                  pl.BlockSpec((None, d, D_GATE), lambda i, j: (l, 0, 0))],
        out_specs=[pl.BlockSpec((tm, tn), lambda i, j: (i, j)),
                   pl.BlockSpec((tm, D_GATE), lambda i, j: (jnp.where(j >= 1, i, jnp.maximum(i - 1, 0)), 0))]
                  + x0_specs,
        out_shape=[jax.ShapeDtypeStruct((t, D_MAIN), BF16),
                   jax.ShapeDtypeStruct((t, D_GATE), F32)] + x0_shape,
        scratch_shapes=[pltpu.VMEM((tm, d), BF16)],
        compiler_params=_cparams(("arbitrary", "arbitrary")),
        name="inproj",
    )(x, *pos_args, mods, g, w_main, w_gate)


R_LOCAL = 0
R_BLOCK = SCAN_C
R_PRE = R_BLOCK + N_SUB
R_POST = R_PRE + N_SUB
R_LEVEL = R_POST + N_SUB
R_TOTAL = R_LEVEL + len(LEVELS) * N_SUB
R_ROWS = -(-(R_TOTAL + N_SUB) // 16) * 16


def _scan_matrices():
    pos = np.arange(SCAN_C)
    blk = pos // SUB
    out = np.zeros((2, R_ROWS, SCAN_C), np.float32)
    for d, fwd in enumerate((True, False)):
        order = np.arange(N_SUB) if fwd else N_SUB - 1 - np.arange(N_SUB)
        earlier_eq = (pos[None, :] <= pos[:, None]) if fwd else (pos[None, :] >= pos[:, None])
        out[d, R_LOCAL:R_LOCAL + SCAN_C] = (blk[:, None] == blk[None, :]) & earlier_eq
        um = order[:, None]
        un = order[None, :]

        def put(row0, cond):
            out[d, row0:row0 + N_SUB] = cond[:, blk]

        put(R_BLOCK, um == un)
        put(R_PRE, un < um)
        put(R_POST, un > um)
        for i, bb in enumerate(LEVELS):
            sb = 2 * bb
            w, wn = um % sb, un % sb
            q_side = w >= bb
            inside = np.where(q_side, (wn >= bb) & (wn < w), (wn > w) & (wn < bb))
            put(R_LEVEL + i * N_SUB, ((um // sb) == (un // sb)) & inside)
        put(R_TOTAL, np.ones((N_SUB, N_SUB), bool))
    return jnp.asarray(out, BF16)


def _per_block(x8):
    return jnp.concatenate([jnp.broadcast_to(x8[m:m + 1, :], (SUB, x8.shape[1])) for m in range(N_SUB)], axis=0)


def _div_p2(x, n):
    return lax.shift_right_logical(x, int(n).bit_length() - 1)


def _mod_p2(x, n):
    return lax.bitwise_and(x, n - 1)


class _Problem:
    def __init__(self, fwd, hg, dv, q, k, la, v, prev, s_ref, s_idx, o_ref, o_rows, o_cols):
        self.fwd, self.hg, self.dv = fwd, hg, dv
        self.q, self.k, self.la, self.v = q, k, la, v
        self.prev = prev
        self.s_ref, self.s_idx, self.o_ref, self.o_rows, self.o_cols = s_ref, s_idx, o_ref, o_rows, o_cols


def _stack_heads(b, hg):
    c = b.shape[0]
    dk = GROUP_K // hg
    per_lane = LANE // dk
    zb = jnp.zeros((c, LANE), BF16)
    blocks = []
    for h in range(hg):
        lv = (h * dk) // LANE
        piece = b[:, lv * LANE:(lv + 1) * LANE]
        if per_lane > 1:
            sub_head = _div_p2(lax.broadcasted_iota(jnp.int32, (c, LANE), 1), dk)
            piece = jnp.where(sub_head == (h % per_lane), piece, 0.0)
        piece = piece.astype(BF16)
        blocks.append(jnp.concatenate([piece if j == lv else zb for j in range(GROUP_K // LANE)], axis=1))
    return jnp.concatenate(blocks, axis=0)


def _stage_factors(p, r):
    c = SCAN_C
    local = r[R_LOCAL:R_LOCAL + c]
    block_total = _per_block(r[R_BLOCK:R_BLOCK + N_SUB])
    small = jnp.exp(r[R_PRE:R_TOTAL])
    pre = _per_block(small[0:N_SUB])
    post = _per_block(small[N_SUB:2 * N_SUB])
    ql = p.q * jnp.exp(local)
    kr = p.k * jnp.exp(block_total - local)
    ki = p.k * jnp.exp(-local)
    p.q_t = (ql * pre).astype(BF16)
    p.k_t = (kr * post).astype(BF16)
    ql_b = ql.astype(BF16)
    masked = GROUP_K // p.hg < LANE
    kr_b = kr if masked else kr.astype(BF16)
    p.a_ops = [ql_b]
    p.b_ops = [_stack_heads(ki, p.hg)]
    for i in range(len(LEVELS)):
        f = _per_block(small[(2 + i) * N_SUB:(3 + i) * N_SUB])
        f_b = f.astype(BF16)
        p.a_ops.append(ql_b * f_b)
        p.b_ops.append(_stack_heads(kr_b * (f if masked else f_b), p.hg))
    tot_rows = jnp.concatenate([r[R_TOTAL:R_TOTAL + N_SUB]] * (GROUP_K // N_SUB), axis=0)
    eye = (lax.broadcasted_iota(jnp.int32, (GROUP_K, GROUP_K), 0)
           == lax.broadcasted_iota(jnp.int32, (GROUP_K, GROUP_K), 1))
    p.decay_col = jnp.exp(jnp.sum(jnp.where(eye, tot_rows, 0.0), axis=1, keepdims=True))


def _stage_products(p):
    hg, dv = p.hg, p.dv
    dk = GROUP_K // hg
    s_in = p.s_ref[p.s_idx] if p.prev is None else p.prev.s_out
    p.o_inter = _dot(p.q_t, s_in.astype(BF16))
    kv = _dot_tn(p.k_t, p.v)
    kv_bd = _blockdiag([kv[h * dk:(h + 1) * dk, h * dv:(h + 1) * dv] for h in range(hg)])
    p.s_out = s_in * p.decay_col + kv_bd
    p.prods = [_dot_nt(a, b) for a, b in zip(p.a_ops, p.b_ops)]


def _stage_finish(p):
    c, hg, dv = SCAN_C, p.hg, p.dv
    si = lax.broadcasted_iota(jnp.int32, (c, hg * c), 0)
    sj = _mod_p2(lax.broadcasted_iota(jnp.int32, (c, hg * c), 1), c)
    m = (_div_p2(si, SUB) == _div_p2(sj, SUB)) & ((sj <= si) if p.fwd else (sj >= si))
    scores = jnp.where(m, p.prods[0], 0.0)
    for i, bb in enumerate(LEVELS):
        blk = bb * SUB
        sb = 2 * blk
        i_second = _mod_p2(si, sb) >= blk
        j_second = _mod_p2(sj, sb) >= blk
        same = _div_p2(si, sb) == _div_p2(sj, sb)
        if p.fwd:
            m = same & i_second & jnp.logical_not(j_second)
        else:
            m = same & jnp.logical_not(i_second) & j_second
        scores = jnp.where(m, p.prods[1 + i], scores)

    zv = jnp.zeros((c, dv), BF16)
    v_bd = jnp.concatenate(
        [jnp.concatenate([p.v[:, h * dv:(h + 1) * dv] if j == h else zv for j in range(hg)], axis=1)
         for h in range(hg)], axis=0)
    o = p.o_inter + _dot(scores.astype(BF16), v_bd)
    p.o_ref[p.o_rows, p.o_cols] = o.astype(p.o_ref.dtype)


def _blockdiag(blocks):
    hg = len(blocks)
    z = jnp.zeros_like(blocks[0])
    return jnp.concatenate(
        [jnp.concatenate([blocks[h] if j == h else z for j in range(hg)], axis=1) for h in range(hg)], axis=0)


def _scan_kernel(*refs, has_init, emit_final, n_sub):
    tiles = (refs[0:6], refs[6:12])
    mat_ref, lrw_ref, lrb_ref, lb_ref = refs[12:16]
    pos = 16
    if has_init:
        sg0_ref, sh0_ref = refs[pos:pos + 2]
        pos += 2
    og_refs = refs[pos:pos + 2]
    oh_refs = refs[pos + 2:pos + 4]
    pos += 4
    if emit_final:
        ng_ref, nh_ref = refs[pos:pos + 2]
        pos += 2
    sg_ref, sh_ref = refs[pos:pos + 2]

    ci = pl.program_id(1)

    @pl.when(ci == 0)
    def _():
        if has_init:
            for d in range(2):
                sg_ref[d] = _blockdiag([sg0_ref[d, h] for h in range(GLA_H)])
                for g in range(HG_GROUPS):
                    sh_ref[d, g] = _blockdiag([sh0_ref[d, g * HG_GH + h] for h in range(HG_GH)])
        else:
            sg_ref[...] = jnp.zeros_like(sg_ref)
            sh_ref[...] = jnp.zeros_like(sh_ref)

    waves = [[] for _ in range(n_sub)]
    for d in range(2):
        fwd = d == 0
        gqk_ref, gv_ref, gz_ref, hq_ref, hf_ref, hi_ref = tiles[d]
        prev = [None] * (1 + HG_GROUPS)
        for n in range(n_sub):
            sub = n if fwd else n_sub - 1 - n
            rows = slice(sub * SCAN_C, (sub + 1) * SCAN_C)
            z = _dot(gz_ref[rows, :].astype(BF16), lrw_ref[d]) + lrb_ref[d]
            group = [_Problem(fwd, GLA_H, GLA_DV,
                              gqk_ref[rows, :GLA_QK].astype(F32) * (GLA_DK ** -0.5),
                              gqk_ref[rows, GLA_QK:].astype(F32),
                              _log_sigmoid(z) * (1.0 / GLA_GATE_NORM), gv_ref[rows, :],
                              prev[0], sg_ref, (d,), og_refs[d], rows, slice(0, GLA_V))]
            for g in range(HG_GROUPS):
                ks = slice(g * GROUP_K, (g + 1) * GROUP_K)
                vs = slice(g * HG_GV, (g + 1) * HG_GV)
                lb = lb_ref[d, :, ks]
                sig = 1.0 / (1.0 + jnp.exp(-hf_ref[rows, ks]))
                group.append(_Problem(fwd, HG_GH, HG_DV,
                                      hq_ref[rows, ks].astype(F32) * (HG_DK ** -0.5), (1.0 - lb) * (1.0 - sig),
                                      jnp.log(lb + (1.0 - lb) * sig), hi_ref[rows, vs],
                                      prev[1 + g], sh_ref, (d, g), oh_refs[d], rows, vs))
            la = jnp.concatenate([p.la for p in group], axis=1)
            la_hi = la.astype(BF16)
            la_lo = (la - la_hi.astype(F32)).astype(BF16)
            r = _dot(mat_ref[d], la_hi) + _dot(mat_ref[d], la_lo)
            for i, p in enumerate(group):
                p.r = r[:, i * GROUP_K:(i + 1) * GROUP_K]
            waves[n] += group
            prev = group

    probs = [p for wave in waves for p in wave]
    for p in probs:
        _stage_factors(p, p.r)
    for p in probs:
        _stage_products(p)
    for p in probs:
        _stage_finish(p)
    last_wave = waves[-1]
    for p in last_wave:
        p.s_ref[p.s_idx] = p.s_out

    if emit_final:
        @pl.when(ci == pl.num_programs(1) - 1)
        def _():
            for d in range(2):
                s = last_wave[3 * d].s_out
                for h in range(GLA_H):
                    ng_ref[d, h] = s[h * GLA_DK:(h + 1) * GLA_DK, h * GLA_DV:(h + 1) * GLA_DV]
                for g in range(HG_GROUPS):
                    s = last_wave[3 * d + 1 + g].s_out
                    for h in range(HG_GH):
                        nh_ref[d, g * HG_GH + h] = s[h * HG_DK:(h + 1) * HG_DK, h * HG_DV:(h + 1) * HG_DV]


def _scan(proj, gates, mats, lrw_pad, lrb, lbs, s0_gla, s0_hg, *, l, n_seq, emit_final):
    t = proj.shape[0]
    n_sub = min(SCAN_STEP, t // n_seq // SCAN_C)
    rows = n_sub * SCAN_C
    nc = t // n_seq // rows
    has_init = s0_gla is not None

    def row(d):
        if d == 0:
            return lambda s, c: s * nc + c
        return lambda s, c: s * nc + (nc - 1 - c)

    def col(r, g):
        return lambda s, c: (r(s, c), g)

    in_specs, args = [], []
    for d in range(2):
        r = row(d)
        in_specs += [pl.BlockSpec((rows, 512), col(r, G_GQK)),
                     pl.BlockSpec((rows, 512), col(r, G_GV)),
                     pl.BlockSpec((rows, LANE), col(r, G_GZ)),
                     pl.BlockSpec((rows, 512), col(r, G_HQ)),
                     pl.BlockSpec((rows, 512), col(r, G_HFF if d == 0 else G_HFB)),
                     pl.BlockSpec((rows, 512), col(r, G_HI))]
        args += [proj, proj, gates, proj, gates, proj]
    in_specs += [pl.BlockSpec((2, R_ROWS, SCAN_C), lambda s, c: (0, 0, 0)),
                 pl.BlockSpec((None, 2, LANE, GLA_QK), lambda s, c: (l, 0, 0, 0)),
                 pl.BlockSpec((None, 2, 1, GLA_QK), lambda s, c: (l, 0, 0, 0)),
                 pl.BlockSpec((2, None, 1, HG_K), lambda s, c: (0, l, 0, 0))]
    args += [mats, lrw_pad, lrb, lbs]
    if has_init:
        in_specs += [pl.BlockSpec((None, None, 2, GLA_H, GLA_DK, GLA_DV), lambda s, c: (s, l, 0, 0, 0, 0)),
                     pl.BlockSpec((None, None, 2, HG_H, HG_DK, HG_DV), lambda s, c: (s, l, 0, 0, 0, 0))]
        args += [s0_gla, s0_hg]
    out_specs = [pl.BlockSpec((rows, 512), col(row(d), 0)) for d in (0, 1, 0, 1)]
    out_shape = [jax.ShapeDtypeStruct((t, 512), BF16)] * 4
    if emit_final:
        out_specs += [pl.BlockSpec((None, 2, GLA_H, GLA_DK, GLA_DV), lambda s, c: (s, 0, 0, 0, 0)),
                      pl.BlockSpec((None, 2, HG_H, HG_DK, HG_DV), lambda s, c: (s, 0, 0, 0, 0))]
        out_shape += [jax.ShapeDtypeStruct((n_seq, 2, GLA_H, GLA_DK, GLA_DV), F32),
                      jax.ShapeDtypeStruct((n_seq, 2, HG_H, HG_DK, HG_DV), F32)]
    return pl.pallas_call(
        functools.partial(_scan_kernel, has_init=has_init, emit_final=emit_final, n_sub=n_sub),
        grid=(n_seq, nc),
        in_specs=in_specs,
        out_specs=out_specs,
        out_shape=out_shape,
        scratch_shapes=[pltpu.VMEM((2, GROUP_K, GLA_V), F32),
                        pltpu.VMEM((2, HG_GROUPS, GROUP_K, HG_GV), F32)],
        compiler_params=_cparams(("parallel", "arbitrary")),
        name="scan",
    )(*args)


def _head_norm_gate(o_fwd, o_bwd, g, gate, n_heads, dv):
    o = (o_fwd + o_bwd).astype(F32)
    half = 0.5 * gate
    silu = (half + half * jnp.tanh(half)).astype(F32)
    parts = []
    for h in range(n_heads):
        oh = o[:, h * dv:(h + 1) * dv]
        parts.append(oh * lax.rsqrt(jnp.mean(oh * oh, axis=-1, keepdims=True) + EPS))
    return jnp.concatenate(parts, axis=1) * g * silu


def _merge_kernel(x_ref, ogf_ref, ogb_ref, ohf_ref, ohb_ref, go_ref, ho_ref, muv_ref, agla_ref, ahg_ref, agm_ref,
                  mod_ref, gng_ref, hng_ref, gmg_ref, ws_ref, bias_ref, wbg_ref, wbh_ref, wbm_ref, wout_ref,
                  out_ref):
    o_gla = _head_norm_gate(ogf_ref[...], ogb_ref[...], gng_ref[...], go_ref[...], GLA_H, GLA_DV)
    o_hg = _head_norm_gate(ohf_ref[...], ohb_ref[...], hng_ref[...], ho_ref[...], HG_H, HG_DV)

    u = _gelu_tanh(muv_ref[:, :GM_W].astype(F32))
    v = _gelu_tanh(muv_ref[:, GM_W:].astype(F32))
    mu = jnp.mean(v, axis=-1, keepdims=True)
    vc = v - mu
    var = jnp.mean(vc * vc, axis=-1, keepdims=True)
    vn = ((vc * lax.rsqrt(var + EPS)) * gmg_ref[...]).astype(BF16)
    rows = []
    for n in range(vn.shape[0] // GM_CHUNK):
        cols = [_dot(ws_ref[g], vn[n * GM_CHUNK:(n + 1) * GM_CHUNK, g * GM_CH:(g + 1) * GM_CH]) for g in range(GM_G)]
        rows.append(jnp.concatenate(cols, axis=1) + bias_ref[...])
    mixed = rows[0] if len(rows) == 1 else jnp.concatenate(rows, axis=0)
    o_gm = u * mixed

    merged = (_sigmoid(agla_ref[...]) * _dot(o_gla.astype(BF16), wbg_ref[...]).astype(BF16)
              + _sigmoid(ahg_ref[...]) * _dot(o_hg.astype(BF16), wbh_ref[...]).astype(BF16)
              + _sigmoid(agm_ref[...]) * _dot(o_gm.astype(BF16), wbm_ref[...]).astype(BF16))
    mix = _dot(merged, wout_ref[...])
    out_ref[...] = x_ref[...] + mod_ref[2:3, :] * mix


def _merge(x, og_f, og_b, oh_f, oh_b, proj, mods, p, l, mod_row, tm):
    t, d = x.shape
    row_spec = lambda w, col: pl.BlockSpec((tm, w), lambda i: (i, col))
    lay = lambda *shape: pl.BlockSpec((None,) + shape, lambda i: (l,) + (0,) * len(shape))
    return pl.pallas_call(
        _merge_kernel,
        grid=(t // tm,),
        in_specs=[row_spec(d, 0),
                  row_spec(GLA_V, 0), row_spec(GLA_V, 0), row_spec(HG_V, 0), row_spec(HG_V, 0),
                  row_spec(512, G_GO), row_spec(512, G_HO), row_spec(1024, G_MUV),
                  row_spec(1024, G_AGLA), row_spec(1024, G_AHG), row_spec(1024, G_AGM),
                  pl.BlockSpec((None, None, 6, d), lambda i: (l, mod_row(i), 0, 0)),
                  lay(1, GLA_V), lay(1, HG_V), lay(1, GM_W),
                  lay(GM_G, GM_CHUNK, GM_CHUNK), lay(GM_CHUNK, GM_W),
                  lay(GLA_V, d), lay(HG_V, d), lay(GM_W, d), lay(d, d)],
        out_specs=pl.BlockSpec((tm, d), lambda i: (i, 0)),
        out_shape=jax.ShapeDtypeStruct((t, d), F32),
        compiler_params=_cparams(("parallel",)),
        name="merge",
    )(x, og_f, og_b, oh_f, oh_b, proj, proj, proj, proj, proj, proj, mods,
      p["gla_norm_g"], p["hg_norm_g"], p["gm_norm_g"], p["gm_ws"], p["gm_bias"],
      p["w_br_gla"], p["w_br_hg"], p["w_br_gm"], p["w_out"])


def _ffn_kernel(x_ref, mod_ref, g_ref, w1_ref, w2_ref, *rest, final):
    if final:
        gfin_ref, out_ref, h_ref, acc_ref = rest
    else:
        out_ref, h_ref, acc_ref = rest
    j = pl.program_id(1)

    @pl.when(j == 0)
    def _():
        x = x_ref[...]
        ms = jnp.mean(x * x, axis=-1, keepdims=True)
        hn = (x * lax.rsqrt(ms + EPS)) * g_ref[...]
        h_ref[...] = (hn * (1.0 + mod_ref[4:5, :]) + mod_ref[3:4, :]).astype(BF16)
        acc_ref[...] = jnp.zeros_like(acc_ref)

    a = jnp.maximum(_dot(h_ref[...], w1_ref[...]), 0.0)
    acc_ref[...] += _dot((a * a).astype(BF16), w2_ref[...])

    @pl.when(j == pl.num_programs(1) - 1)
    def _():
        y = x_ref[...] + mod_ref[5:6, :] * acc_ref[...]
        if final:
            y = (y * lax.rsqrt(jnp.mean(y * y, axis=-1, keepdims=True) + EPS)) * gfin_ref[...]
        out_ref[...] = y


def _ffn(x, mods, g, w1, w2, l, mod_row, tm, final_g=None):
    t, d = x.shape
    tf = 1024
    final = final_g is not None
    extra_specs = [pl.BlockSpec((1, d), lambda i, j: (0, 0))] if final else []
    extra_args = [final_g.reshape(1, d)] if final else []
    return pl.pallas_call(
        functools.partial(_ffn_kernel, final=final),
        grid=(t // tm, D_FF // tf),
        in_specs=[pl.BlockSpec((tm, d), lambda i, j: (i, 0)),
                  pl.BlockSpec((None, None, 6, d), lambda i, j: (l, mod_row(i), 0, 0)),
                  pl.BlockSpec((None, 1, d), lambda i, j: (l, 0, 0)),
                  pl.BlockSpec((None, d, tf), lambda i, j: (l, 0, j)),
                  pl.BlockSpec((None, tf, d), lambda i, j: (l, j, 0))] + extra_specs,
        out_specs=pl.BlockSpec((tm, d), lambda i, j: (i, 0)),
        out_shape=jax.ShapeDtypeStruct((t, d), F32),
        scratch_shapes=[pltpu.VMEM((tm, d), BF16), pltpu.VMEM((tm, d), F32)],
        compiler_params=_cparams(("parallel", "arbitrary")),
        name="ffn",
    )(x, mods, g, w1, w2, *extra_args)


def _grid_posemb(n_tok):
    rows = n_tok // GRID_W
    r, col = jnp.meshgrid(jnp.arange(rows, dtype=F32), jnp.arange(GRID_W, dtype=F32), indexing="ij")
    r = r.reshape(-1)
    col = col.reshape(-1)
    nf = D_MODEL // 4
    omega = 1.0 / (10000.0 ** (jnp.arange(nf, dtype=F32) / nf))
    er = r[:, None] * omega
    ec = col[:, None] * omega
    return jnp.concatenate([jnp.sin(er), jnp.cos(er), jnp.sin(ec), jnp.cos(ec)], axis=-1)


def _col_ranges(names):
    starts = dict(zip(SPLIT_NAMES, np.cumsum((0,) + SPLIT_SIZES[:-1])))
    sizes = dict(zip(SPLIT_NAMES, SPLIT_SIZES))
    ranges = []
    for n in names:
        a, b = int(starts[n]), int(starts[n]) + sizes[n]
        if ranges and ranges[-1][1] == a:
            ranges[-1] = (ranges[-1][0], b)
        else:
            ranges.append((a, b))
    return ranges


def _tile_rows(n_rows, seq_rows, shared_mod):
    return min(ROW_TILE, n_rows if shared_mod else seq_rows)


def kernel(x_prompt, x_sample, c, state_gla, state_hgrn, c_ctx, ada_w, ada_b, norm_mix_g, norm_ffn_g, w_in, gla_lr_w,
           gla_lr_b, gla_norm_g, hg_lb_logits, hg_norm_g, gm_norm_g, gm_ws, gm_bs, w_br_gla, w_br_hg, w_br_gm, w_out,
           w_ff1, w_ff2, final_norm_g):
    batch, seq, d = x_prompt.shape
    dec_batch, dec_seq, _ = x_sample.shape
    depth = ada_w.shape[0]

    w_main, w_gate = _regroup_w_in(w_in)
    lrw_pad = jnp.stack(
        [jnp.pad(gla_lr_w[:, dd], ((0, 0), (dd * GLA_RANK, LANE - (dd + 1) * GLA_RANK), (0, 0))) for dd in range(2)],
        axis=1).astype(BF16)
    lrb = gla_lr_b.reshape(depth, 2, 1, GLA_QK)
    gm_bias = jnp.broadcast_to(jnp.swapaxes(gm_bs, 1, 2)[:, :, :, None],
                               (depth, GM_CHUNK, GM_G, GM_CH)).reshape(depth, GM_CHUNK, GM_W)
    params = {
        "gla_norm_g": gla_norm_g.reshape(depth, 1, GLA_V),
        "hg_norm_g": hg_norm_g.reshape(depth, 1, HG_V),
        "gm_norm_g": gm_norm_g.reshape(depth, 1, GM_W),
        "gm_ws": gm_ws.astype(BF16),
        "gm_bias": gm_bias,
        "w_br_gla": w_br_gla.astype(BF16),
        "w_br_hg": w_br_hg.astype(BF16),
        "w_br_gm": w_br_gm.astype(BF16),
        "w_out": w_out.astype(BF16),
    }
    w1 = w_ff1.astype(BF16)
    w2 = w_ff2.astype(BF16)
    g_mix = norm_mix_g.reshape(depth, 1, d)
    g_ffn = norm_ffn_g.reshape(depth, 1, d)
    mats = _scan_matrices()

    n_mod = 1 + dec_batch
    n_mod_pad = -(-n_mod // 8) * 8
    cvec = jnp.concatenate([c_ctx[None, :], c, jnp.zeros((n_mod_pad - n_mod, d), F32)], axis=0)
    mods = _modulations(cvec, ada_w, ada_b).reshape(depth, n_mod_pad, 6, d)
    lbs = _lower_bounds(hg_lb_logits).reshape(2, depth, 1, HG_K)

    xc = x_prompt.reshape(batch * seq, d)
    xl = x_sample.reshape(dec_batch * dec_seq, d)
    pos = _grid_posemb(dec_seq)

    tm_c = _tile_rows(batch * seq, seq, True)
    tm_l = _tile_rows(dec_batch * dec_seq, dec_seq, False)
    tmm_c = min(MERGE_TILE, batch * seq)
    tmm_l = min(MERGE_TILE, dec_seq)
    ctx_row = lambda i: 0
    lat_row = lambda tm: (lambda i: 1 + (i * tm) // dec_seq)
    new_gla = []
    new_hg = []
    for l in range(depth):
        proj, gz = _inproj(xc, mods, g_mix, w_main, w_gate, l, ctx_row, tm_c)
        og_f, og_b, oh_f, oh_b, ng, nh = _scan(proj, gz, mats, lrw_pad, lrb, lbs, None, None, l=l, n_seq=batch,
                                               emit_final=True)
        new_gla.append(ng)
        new_hg.append(nh)
        xc = _merge(xc, og_f, og_b, oh_f, oh_b, proj, mods, params, l, ctx_row, tmm_c)
        fin = final_norm_g if l == depth - 1 else None
        xc = _ffn(xc, mods, g_ffn, w1, w2, l, ctx_row, tm_c, fin)

        if l == 0:
            proj, gz, xl = _inproj(xl, mods, g_mix, w_main, w_gate, l, lat_row(tm_l), tm_l, pos)
        else:
            proj, gz = _inproj(xl, mods, g_mix, w_main, w_gate, l, lat_row(tm_l), tm_l)
        og_f, og_b, oh_f, oh_b = _scan(proj, gz, mats, lrw_pad, lrb, lbs, state_gla, state_hgrn, l=l,
                                       n_seq=dec_batch, emit_final=False)
        xl = _merge(xl, og_f, og_b, oh_f, oh_b, proj, mods, params, l, lat_row(tmm_l), tmm_l)
        xl = _ffn(xl, mods, g_ffn, w1, w2, l, lat_row(tm_l), tm_l, fin)

    return (xc.reshape(batch, seq, d), xl.reshape(dec_batch, dec_seq, d), jnp.stack(new_gla, axis=1), jnp.stack(new_hg, axis=1))
```
